```python
import jax, jax.numpy as jnp
from jax import lax
import numpy as np

D_MODEL = 2048
BATCH = 16
SEQ = 2048
DEPTH = 4

N_MIXERS = 2
N_HGRN_LAYERS = (DEPTH + 1) // 2
N_GDN_LAYERS = DEPTH // 2

HG_DK = 128
HG_HEADS = D_MODEL // HG_DK
HG_DV = D_MODEL // HG_HEADS
HG_CHUNK = 16
HG_IN = 2 * HG_HEADS * HG_DK + 2 * HG_HEADS * HG_DV

GDN_DK = 128
GDN_DV = 128
GDN_K_HEADS = D_MODEL // 128
GDN_V_HEADS = 2 * GDN_K_HEADS
GDN_KEY_DIM = GDN_K_HEADS * GDN_DK
GDN_VAL_DIM = GDN_V_HEADS * GDN_DV
GDN_CONV = 4
GDN_CONV_DIM = 2 * GDN_KEY_DIM + GDN_VAL_DIM
GDN_IN = GDN_CONV_DIM + GDN_VAL_DIM + 2 * GDN_V_HEADS
GDN_CHUNK = 64

FFN_HIDDEN = -(-(8 * D_MODEL) // (3 * 256)) * 256
N_MOD = 6
EPS = 1e-6

kernel_name = "hybrid_hgrn2_gdn_adaln_trunk"


def rms_norm(x, w, eps=EPS):
    xf = x.astype(jnp.float32)
    y = xf * lax.rsqrt(jnp.mean(xf * xf, axis=-1, keepdims=True) + eps)
    return (y * w.astype(jnp.float32)).astype(x.dtype)


def l2_normalize(x, eps=EPS):
    xf = x.astype(jnp.float32)
    return xf * lax.rsqrt(jnp.sum(xf * xf, axis=-1, keepdims=True) + eps)


def causal_depthwise_conv(x, w):
    K = w.shape[0]
    S = x.shape[1]
    xp = jnp.pad(x, ((0, 0), (K - 1, 0), (0, 0)))
    out = xp[:, 0:S, :] * w[0]
    for j in range(1, K):
        out = out + xp[:, j:j + S, :] * w[j]
    return out


def hgrn2_chunked(q, k, v, log_f):
    B, S, H, DK = q.shape
    DV = v.shape[-1]
    C = HG_CHUNK
    N = S // C

    def to_chunks(t):
        return t.reshape(B, N, C, H, t.shape[-1]).transpose(1, 0, 3, 2, 4)

    qc, kc, vc, gc = to_chunks(q), to_chunks(k), to_chunks(v), to_chunks(log_f)
    causal = jnp.tril(jnp.ones((C, C), dtype=bool))

    def step(state, inp):
        q_n, k_n, v_n, g_n = inp
        b = jnp.cumsum(g_n, axis=-2)
        b_last = b[..., -1:, :]
        o_inter = jnp.einsum('bhtk,bhkv->bhtv', q_n * jnp.exp(b), state)
        diff = jnp.where(causal[:, :, None], b[..., :, None, :] - b[..., None, :, :], -jnp.inf)
        scores = jnp.einsum('bhtk,bhsk,bhtsk->bhts', q_n, k_n, jnp.exp(diff))
        o = o_inter + jnp.einsum('bhts,bhsv->bhtv', scores, v_n)
        state = state * jnp.exp(b_last[..., 0, :])[..., None] + jnp.einsum(
            'bhsk,bhsv->bhkv', k_n * jnp.exp(b_last - b), v_n)
        return state, o

    state0 = jnp.zeros((B, H, DK, DV), jnp.float32)
    _, oc = lax.scan(step, state0, (qc, kc, vc, gc))
    return oc.transpose(1, 0, 3, 2, 4).reshape(B, S, H, DV)


def hgrn2_mixer(h, w_in, lower_bound, norm_w, w_out):
    B, S, _ = h.shape
    proj = h @ w_in
    hk = HG_HEADS * HG_DK
    hv = HG_HEADS * HG_DV
    q = jax.nn.silu(proj[..., :hk].astype(jnp.float32))
    f_logit = proj[..., hk:2 * hk].astype(jnp.float32)
    i_val = proj[..., 2 * hk:2 * hk + hv].astype(jnp.float32)
    out_gate = proj[..., 2 * hk + hv:].astype(jnp.float32)
    lb = lower_bound.astype(jnp.float32)
    log_f = jnp.logaddexp(jnp.log(lb), jnp.log1p(-lb) + jax.nn.log_sigmoid(f_logit))
    k = -jnp.expm1(log_f)
    o = hgrn2_chunked(q.reshape(B, S, HG_HEADS, HG_DK), k.reshape(B, S, HG_HEADS, HG_DK),
                      i_val.reshape(B, S, HG_HEADS, HG_DV), log_f.reshape(B, S, HG_HEADS, HG_DK))
    o = rms_norm(o, norm_w) * jax.nn.sigmoid(out_gate.reshape(B, S, HG_HEADS, HG_DV))
    return o.reshape(B, S, hv).astype(h.dtype) @ w_out


def gated_delta_chunked(q, k, v, beta, g):
    B, S, H, DK = q.shape
    DV = v.shape[-1]
    C = GDN_CHUNK
    N = S // C

    def chunk_vec(t):
        return t.reshape(B, N, C, H, t.shape[-1]).transpose(0, 3, 1, 2, 4)

    def chunk_sca(t):
        return t.reshape(B, N, C, H).transpose(0, 3, 1, 2)

    qc, kc, vc = chunk_vec(q), chunk_vec(k), chunk_vec(v)
    bc, gcum = chunk_sca(beta), jnp.cumsum(chunk_sca(g), axis=-1)
    causal = jnp.tril(jnp.ones((C, C), dtype=bool))
    strict = jnp.tril(jnp.ones((C, C), dtype=bool), -1)
    decay = jnp.exp(jnp.where(causal, gcum[..., :, None] - gcum[..., None, :], -jnp.inf))
    kk = jnp.einsum('bhntk,bhnsk->bhnts', kc, kc)
    a_low = jnp.where(strict, bc[..., :, None] * kk * decay, 0.0)
    eye = jnp.broadcast_to(jnp.eye(C, dtype=jnp.float32), a_low.shape)
    t_inv = lax.linalg.triangular_solve(a_low, eye, left_side=True, lower=True, unit_diagonal=True)
    u = jnp.einsum('bhnts,bhnsv->bhntv', t_inv, vc * bc[..., None])
    w = jnp.einsum('bhnts,bhnsk->bhntk', t_inv, kc * (bc * jnp.exp(gcum))[..., None])
    attn = jnp.where(causal, jnp.einsum('bhntk,bhnsk->bhnts', qc, kc) * decay, 0.0)

    def step(state, inp):
        q_n, k_n, u_n, w_n, attn_n, g_n = inp
        v_new = u_n - jnp.einsum('bhck,bhkv->bhcv', w_n, state)
        o = jnp.einsum('bhck,bhkv->bhcv', q_n * jnp.exp(g_n)[..., None], state) + jnp.einsum(
            'bhts,bhsv->bhtv', attn_n, v_new)
        g_last = g_n[..., -1:]
        state = state * jnp.exp(g_last)[..., None] + jnp.einsum(
            'bhck,bhcv->bhkv', k_n * jnp.exp(g_last - g_n)[..., None], v_new)
        return state, o

    mv = lambda t: jnp.moveaxis(t, 2, 0)
    state0 = jnp.zeros((B, H, DK, DV), jnp.float32)
    _, oc = lax.scan(step, state0, (mv(qc), mv(kc), mv(u), mv(w), mv(attn), mv(gcum)))
    return jnp.moveaxis(oc, 0, 2).transpose(0, 2, 3, 1, 4).reshape(B, S, H, DV)


def gdn_mixer(h, w_in, conv_w, a_log, dt_bias, norm_w, w_out):
    B, S, _ = h.shape
    proj = h @ w_in
    qkv = jax.nn.silu(causal_depthwise_conv(proj[..., :GDN_CONV_DIM], conv_w)).astype(jnp.float32)
    z = proj[..., GDN_CONV_DIM:GDN_CONV_DIM + GDN_VAL_DIM].astype(jnp.float32)
    b_logit = proj[..., GDN_CONV_DIM + GDN_VAL_DIM:GDN_CONV_DIM + GDN_VAL_DIM + GDN_V_HEADS].astype(jnp.float32)
    a_in = proj[..., GDN_CONV_DIM + GDN_VAL_DIM + GDN_V_HEADS:].astype(jnp.float32)
    rep = GDN_V_HEADS // GDN_K_HEADS
    q = l2_normalize(qkv[..., :GDN_KEY_DIM].reshape(B, S, GDN_K_HEADS, GDN_DK))
    k = l2_normalize(qkv[..., GDN_KEY_DIM:2 * GDN_KEY_DIM].reshape(B, S, GDN_K_HEADS, GDN_DK))
    v = qkv[..., 2 * GDN_KEY_DIM:].reshape(B, S, GDN_V_HEADS, GDN_DV)
    q = jnp.repeat(q, rep, axis=2) * (GDN_DK ** -0.5)
    k = jnp.repeat(k, rep, axis=2)
    beta = jax.nn.sigmoid(b_logit)
    g = -jnp.exp(a_log.astype(jnp.float32)) * jax.nn.softplus(a_in + dt_bias.astype(jnp.float32))
    o = gated_delta_chunked(q, k, v, beta, g)
    o = rms_norm(o, norm_w) * jax.nn.silu(z.reshape(B, S, GDN_V_HEADS, GDN_DV))
    return o.reshape(B, S, GDN_VAL_DIM).astype(h.dtype) @ w_out


def swiglu_ffn(h, w_gate_up, w_down):
    gu = h @ w_gate_up
    return (jax.nn.silu(gu[..., :FFN_HIDDEN]) * gu[..., FFN_HIDDEN:]) @ w_down


def setup_inputs(seed: int = 0) -> dict:
    key = jax.random.key(seed)
    ks = jax.random.split(key, 20)
    f32 = jnp.float32

    def nrm(k, shape, scale):
        return jax.random.normal(k, shape, f32) * scale

    dt = jnp.exp(jax.random.uniform(ks[13], (N_GDN_LAYERS, GDN_V_HEADS), f32,
                                    np.log(1e-3).astype(np.float32), np.log(1e-1).astype(np.float32)))
    return {
        "x": nrm(ks[0], (BATCH, SEQ, D_MODEL), 1.0),
        "c": nrm(ks[1], (BATCH, D_MODEL), 1.0),
        "ada_w": nrm(ks[2], (DEPTH, D_MODEL, N_MOD * D_MODEL), 0.5 * D_MODEL ** -0.5),
        "ada_b": nrm(ks[3], (DEPTH, N_MOD * D_MODEL), 0.01),
        "norm_w": 1.0 + nrm(ks[4], (DEPTH, 4, D_MODEL), 0.02),
        "hg_w_in": nrm(ks[5], (N_HGRN_LAYERS, D_MODEL, HG_IN), D_MODEL ** -0.5),
        "hg_lb_logits": nrm(ks[6], (N_HGRN_LAYERS, HG_HEADS * HG_DK), 0.5),
        "hg_norm_w": 1.0 + nrm(ks[7], (N_HGRN_LAYERS, HG_DV), 0.02),
        "hg_w_out": nrm(ks[8], (N_HGRN_LAYERS, HG_HEADS * HG_DV, D_MODEL), (HG_HEADS * HG_DV) ** -0.5),
        "gdn_w_in": nrm(ks[9], (N_GDN_LAYERS, D_MODEL, GDN_IN), D_MODEL ** -0.5),
        "gdn_conv_w": nrm(ks[10], (N_GDN_LAYERS, GDN_CONV, GDN_CONV_DIM), GDN_CONV ** -0.5),
        "gdn_A_log": jnp.log(jax.random.uniform(ks[11], (N_GDN_LAYERS, GDN_V_HEADS), f32, 1.0, 16.0)),
        "gdn_dt_bias": dt + jnp.log(-jnp.expm1(-dt)),
        "gdn_norm_w": 1.0 + nrm(ks[12], (N_GDN_LAYERS, GDN_DV), 0.02),
        "gdn_w_out": nrm(ks[14], (N_GDN_LAYERS, GDN_VAL_DIM, D_MODEL), GDN_VAL_DIM ** -0.5),
        "ffn_w_gate_up": nrm(ks[15], (DEPTH, D_MODEL, 2 * FFN_HIDDEN), D_MODEL ** -0.5),
        "ffn_w_down": nrm(ks[16], (DEPTH, FFN_HIDDEN, D_MODEL), FFN_HIDDEN ** -0.5),
    }


def reference(x, c, ada_w, ada_b, norm_w, hg_w_in, hg_lb_logits, hg_norm_w, hg_w_out,
              gdn_w_in, gdn_conv_w, gdn_A_log, gdn_dt_bias, gdn_norm_w, gdn_w_out,
              ffn_w_gate_up, ffn_w_down):
    lb_table = jnp.cumsum(jax.nn.softmax(hg_lb_logits.astype(jnp.float32), axis=0), axis=0)
    lb_table = lb_table - lb_table[:1]
    c_act = jax.nn.silu(c)
    for layer in range(DEPTH):
        mod = c_act @ ada_w[layer] + ada_b[layer]
        shift_m, scale_m, gate_m, shift_f, scale_f, gate_f = jnp.split(mod[:, None, :], N_MOD, axis=-1)
        h = rms_norm(x, norm_w[layer, 0]) * (1.0 + scale_m) + shift_m
        j = layer // N_MIXERS
        if layer % N_MIXERS == 0:
            y = hgrn2_mixer(h, hg_w_in[j], lb_table[j], hg_norm_w[j], hg_w_out[j])
        else:
            y = gdn_mixer(h, gdn_w_in[j], gdn_conv_w[j], gdn_A_log[j], gdn_dt_bias[j],
                          gdn_norm_w[j], gdn_w_out[j])
        x = x + gate_m * rms_norm(y, norm_w[layer, 1])
        h = rms_norm(x, norm_w[layer, 2]) * (1.0 + scale_f) + shift_f
        y = swiglu_ffn(h, ffn_w_gate_up[layer], ffn_w_down[layer])
        x = x + gate_f * rms_norm(y, norm_w[layer, 3])
    return x
```

```python
import functools

import jax
import jax.numpy as jnp
from jax import lax
from jax.experimental import pallas as pl
from jax.experimental.pallas import tpu as pltpu

F32 = jnp.float32
BF16 = jnp.bfloat16
EPS = 1e-6
N_MOD = 6
HEAD_DIM = 128
GDN_CONV = 4
GDN_CHUNK = 64
HG_CHUNK = 128
HG_LEAF = 8
NEG = -1e30
VMEM_LIMIT = 56 * 1024 * 1024


def _cparams(sem):
    return pltpu.CompilerParams(dimension_semantics=sem, vmem_limit_bytes=VMEM_LIMIT)


def _tile(n, pref):
    t = min(n, pref)
    while n % t:
        t //= 2
    return t


def _sigmoid(x):
    return 1.0 / (1.0 + jnp.exp(-x))


def _silu(x):
    return x * _sigmoid(x)


def _bdot(a, b):
    return jnp.dot(a.astype(BF16), b.astype(BF16), preferred_element_type=F32)


def _bdot_nt(a, b):
    return lax.dot_general(a.astype(BF16), b.astype(BF16), (((1,), (1,)), ((), ())),
                           preferred_element_type=F32)


def _bdot_tn(a, b):
    return lax.dot_general(a.astype(BF16), b.astype(BF16), (((0,), (0,)), ((), ())),
                           preferred_element_type=F32)


def _split_dot(a_exact_bf16, x):
    hi = x.astype(BF16)
    lo = (x - hi.astype(F32)).astype(BF16)
    return (jnp.dot(a_exact_bf16, hi, preferred_element_type=F32)
            + jnp.dot(a_exact_bf16, lo, preferred_element_type=F32))


def _rms(x, w):
    ms = jnp.mean(x * x, axis=-1, keepdims=True)
    return x * lax.rsqrt(ms + EPS) * w


def _ada_kernel(c_ref, w_ref, b_ref, o_ref):
    c = c_ref[...]
    o_ref[0] = _bdot(_silu(c), w_ref[0]) + b_ref[0]


def ada_modulation(c, ada_w, ada_b):
    depth, d, n = ada_w.shape
    b = c.shape[0]
    tn = _tile(n, 1024)
    return pl.pallas_call(
        _ada_kernel,
        grid=(depth, n // tn),
        in_specs=[pl.BlockSpec((b, d), lambda l, j: (0, 0)),
                  pl.BlockSpec((1, d, tn), lambda l, j: (l, 0, j)),
                  pl.BlockSpec((1, 1, tn), lambda l, j: (l, 0, j))],
        out_specs=pl.BlockSpec((1, b, tn), lambda l, j: (l, 0, j)),
        out_shape=jax.ShapeDtypeStruct((depth, b, n), F32),
        compiler_params=_cparams(("parallel", "parallel")),
        name="ada_modulation",
    )(c, ada_w, ada_b.reshape(depth, 1, n))


def _norm_mod(x, nw, mod_ref, shift_row):
    scale = mod_ref[0, shift_row + 1:shift_row + 2, :]
    shift = mod_ref[0, shift_row:shift_row + 1, :]
    return _rms(x, nw) * (1.0 + scale) + shift


def _proj_kernel(x_ref, nw_ref, mod_ref, w_ref, o_ref, h_ref, *, shift_row):
    @pl.when(pl.program_id(2) == 0)
    def _():
        h_ref[...] = _norm_mod(x_ref[0], nw_ref[...], mod_ref, shift_row).astype(BF16)

    o_ref[0] = jnp.dot(h_ref[...], w_ref[...], preferred_element_type=F32).astype(o_ref.dtype)


def norm_mod_project(x, nw, mod, w, shift_row, out_dtype):
    b, s, d = x.shape
    n = w.shape[1]
    tm = _tile(s, 1024)
    tn = _tile(n, 1024)
    return pl.pallas_call(
        functools.partial(_proj_kernel, shift_row=shift_row),
        grid=(b, s // tm, n // tn),
        in_specs=[pl.BlockSpec((1, tm, d), lambda i, m, j: (i, m, 0)),
                  pl.BlockSpec((1, d), lambda i, m, j: (0, 0)),
                  pl.BlockSpec((1, N_MOD, d), lambda i, m, j: (i, 0, 0)),
                  pl.BlockSpec((d, tn), lambda i, m, j: (0, j))],
        out_specs=pl.BlockSpec((1, tm, tn), lambda i, m, j: (i, m, j)),
        out_shape=jax.ShapeDtypeStruct((b, s, n), out_dtype),
        scratch_shapes=[pltpu.VMEM((tm, d), BF16)],
        compiler_params=_cparams(("parallel", "parallel", "arbitrary")),
        name="norm_mod_project",
    )(x, nw, mod, w)


def _out_kernel(a_ref, w_ref, x_ref, nw_ref, mod_ref, o_ref, *, gate_row):
    y = jnp.dot(a_ref[0], w_ref[...], preferred_element_type=F32)
    gate = mod_ref[0, gate_row:gate_row + 1, :]
    o_ref[0] = x_ref[0] + gate * _rms(y, nw_ref[...])


def out_project_residual(a, w, x, nw, mod, gate_row):
    b, s, k = a.shape
    d = w.shape[1]
    tm = _tile(s, 512)
    return pl.pallas_call(
        functools.partial(_out_kernel, gate_row=gate_row),
        grid=(b, s // tm),
        in_specs=[pl.BlockSpec((1, tm, k), lambda i, m: (i, m, 0)),
                  pl.BlockSpec((k, d), lambda i, m: (0, 0), pipeline_mode=pl.Buffered(1)),
                  pl.BlockSpec((1, tm, d), lambda i, m: (i, m, 0)),
                  pl.BlockSpec((1, d), lambda i, m: (0, 0)),
                  pl.BlockSpec((1, N_MOD, d), lambda i, m: (i, 0, 0))],
        out_specs=pl.BlockSpec((1, tm, d), lambda i, m: (i, m, 0)),
        out_shape=jax.ShapeDtypeStruct((b, s, d), F32),
        compiler_params=_cparams(("parallel", "parallel")),
        name="out_project_residual",
    )(a, w, x, nw, mod)


def _ffn_kernel(x_ref, nw_in_ref, mod_ref, wg_ref, wu_ref, wd_ref, nw_out_ref, o_ref,
                h_ref, acc_ref):
    k = pl.program_id(2)

    @pl.when(k == 0)
    def _():
        h_ref[...] = _norm_mod(x_ref[0], nw_in_ref[...], mod_ref, 3).astype(BF16)

    h = h_ref[...]
    g = jnp.dot(h, wg_ref[...], preferred_element_type=F32)
    u = jnp.dot(h, wu_ref[...], preferred_element_type=F32)
    part = jnp.dot((_silu(g) * u).astype(BF16), wd_ref[...], preferred_element_type=F32)

    @pl.when(k == 0)
    def _():
        acc_ref[...] = part

    @pl.when(k > 0)
    def _():
        acc_ref[...] += part

    @pl.when(k == pl.num_programs(2) - 1)
    def _():
        gate = mod_ref[0, 5:6, :]
        o_ref[0] = x_ref[0] + gate * _rms(acc_ref[...], nw_out_ref[...])


def ffn_residual(x, nw_in, nw_out, mod, w_gate_up, w_down):
    b, s, d = x.shape
    f = w_down.shape[0]
    tm = _tile(s, 512)
    tf = _tile(f, 512)
    nf = f // tf
    return pl.pallas_call(
        _ffn_kernel,
        grid=(b, s // tm, nf),
        in_specs=[pl.BlockSpec((1, tm, d), lambda i, m, k: (i, m, 0)),
                  pl.BlockSpec((1, d), lambda i, m, k: (0, 0)),
                  pl.BlockSpec((1, N_MOD, d), lambda i, m, k: (i, 0, 0)),
                  pl.BlockSpec((d, tf), lambda i, m, k: (0, k)),
                  pl.BlockSpec((d, tf), lambda i, m, k: (0, k + nf)),
                  pl.BlockSpec((tf, d), lambda i, m, k: (k, 0)),
                  pl.BlockSpec((1, d), lambda i, m, k: (0, 0))],
        out_specs=pl.BlockSpec((1, tm, d), lambda i, m, k: (i, m, 0)),
        out_shape=jax.ShapeDtypeStruct((b, s, d), F32),
        scratch_shapes=[pltpu.VMEM((tm, d), BF16), pltpu.VMEM((tm, d), F32)],
        compiler_params=_cparams(("parallel", "parallel", "arbitrary")),
        name="ffn_residual",
    )(x, nw_in, mod, w_gate_up, w_gate_up, w_down, nw_out)


def _row_group_bcast(x, group, row):
    c, w = x.shape
    x3 = x.reshape(c // group, group, w)
    return jnp.broadcast_to(x3[:, row:row + 1, :], x3.shape).reshape(c, w)


def _hgrn_chunk(q_raw, f_raw, v, log_lb, log_1mlb, state_t, tril):
    c = HG_CHUNK
    q = _silu(q_raw)
    log_sig = jnp.minimum(f_raw, 0.0) - jnp.log1p(jnp.exp(-jnp.abs(f_raw)))
    cand = log_1mlb + log_sig
    log_f = jnp.maximum(log_lb, cand) + jnp.log1p(jnp.exp(-jnp.abs(log_lb - cand)))
    k = 1.0 - jnp.exp(log_f)
    b = _split_dot(tril, log_f)
    b_last = b[c - 1:c, :]

    row = lax.broadcasted_iota(jnp.int32, (c, c), 0)
    col = lax.broadcasted_iota(jnp.int32, (c, c), 1)
    rowk = lax.broadcasted_iota(jnp.int32, (c, HEAD_DIM), 0)

    o = _bdot_nt(q * jnp.exp(b), state_t)
    scores = jnp.zeros((c, c), F32)
    n = c
    while n > HG_LEAF:
        half = n // 2
        r = _row_group_bcast(b, n, half - 1)
        right = (rowk % n) >= half
        qt = q * jnp.exp(jnp.where(right, b - r, NEG))
        kt = k * jnp.exp(jnp.where(right, NEG, r - b))
        same = (row // n) == (col // n)
        scores = scores + jnp.where(same, _bdot_nt(qt, kt), 0.0)
        n = half
    o = o + _bdot(scores, v)

    g = c // HG_LEAF
    q3 = q.reshape(g, HG_LEAF, HEAD_DIM)
    k3 = k.reshape(g, HG_LEAF, HEAD_DIM)
    b3 = b.reshape(g, HG_LEAF, HEAD_DIM)
    v3 = v.reshape(g, HG_LEAF, HEAD_DIM)
    t_in = lax.broadcasted_iota(jnp.int32, (g, HG_LEAF, HEAD_DIM), 1)
    o3 = jnp.zeros((g, HG_LEAF, HEAD_DIM), F32)
    for s in range(HG_LEAF):
        bs = b3[:, s:s + 1, :]
        ks = k3[:, s:s + 1, :]
        vs = v3[:, s:s + 1, :]
        p = q3 * ks * jnp.exp(jnp.where(t_in >= s, b3 - bs, NEG))
        o3 = o3 + jnp.sum(p, axis=-1, keepdims=True) * vs
    o = o + o3.reshape(c, HEAD_DIM)

    kd = k * jnp.exp(b_last - b)
    new_state_t = state_t * jnp.exp(b_last) + _bdot_tn(v, kd)
    return o, new_state_t


def _hgrn_kernel(q_ref, f_ref, v_ref, g_ref, lbl_ref, nw_ref, o_ref, state_ref, *, layer_j):
    @pl.when(pl.program_id(2) == 0)
    def _():
        state_ref[...] = jnp.zeros_like(state_ref)

    logits = lbl_ref[0]
    e = jnp.exp(logits - jnp.max(logits, axis=0, keepdims=True))
    sm = e / jnp.sum(e, axis=0, keepdims=True)
    lb = jnp.sum(sm[:layer_j + 1], axis=0, keepdims=True) - sm[0:1]
    log_lb = jnp.log(lb)
    log_1mlb = jnp.log1p(-lb)

    c = HG_CHUNK
    tril = (lax.broadcasted_iota(jnp.int32, (c, c), 0)
            >= lax.broadcasted_iota(jnp.int32, (c, c), 1)).astype(BF16)
    nw = nw_ref[...]

    def body(i, carry):
        sl = pl.ds(pl.multiple_of(i * c, c), c)
        o, new_state = _hgrn_chunk(q_ref[0, sl, :], f_ref[0, sl, :], v_ref[0, sl, :],
                                   log_lb, log_1mlb, state_ref[...], tril)
        state_ref[...] = new_state
        o_ref[0, sl, :] = (_rms(o, nw) * _sigmoid(g_ref[0, sl, :])).astype(o_ref.dtype)
        return carry

    lax.fori_loop(0, q_ref.shape[1] // c, body, 0)


def hgrn2_recurrence(proj, lb_logits, norm_w, layer_j):
    b, s, n4 = proj.shape
    hd = n4 // 4
    heads = hd // HEAD_DIM
    tc = _tile(s, 512)
    nl = lb_logits.shape[0]
    lbl = lb_logits.astype(F32).reshape(nl, heads, HEAD_DIM).transpose(1, 0, 2)

    def col(section):
        return pl.BlockSpec((1, tc, HEAD_DIM), lambda i, h, t: (i, t, section * heads + h))

    return pl.pallas_call(
        functools.partial(_hgrn_kernel, layer_j=layer_j),
        grid=(b, heads, s // tc),
        in_specs=[col(0), col(1), col(2), col(3),
                  pl.BlockSpec((1, nl, HEAD_DIM), lambda i, h, t: (h, 0, 0)),
                  pl.BlockSpec((1, HEAD_DIM), lambda i, h, t: (0, 0))],
        out_specs=pl.BlockSpec((1, tc, HEAD_DIM), lambda i, h, t: (i, t, h)),
        out_shape=jax.ShapeDtypeStruct((b, s, hd), BF16),
        scratch_shapes=[pltpu.VMEM((HEAD_DIM, HEAD_DIM), F32)],
        compiler_params=_cparams(("parallel", "parallel", "arbitrary")),
        name="hgrn2_recurrence",
    )(proj, proj, proj, proj, lbl, norm_w.astype(F32).reshape(1, HEAD_DIM))


def _causal_conv_silu(x, prev, w):
    rows8 = lax.broadcasted_iota(jnp.int32, prev.shape, 0)
    out = x * w[GDN_CONV - 1:GDN_CONV, :]
    for back in range(1, GDN_CONV):
        shifted = pltpu.roll(x, back, axis=0)
        head = jnp.where(rows8 < back, pltpu.roll(prev, back, axis=0), shifted[0:8, :])
        tap = jnp.concatenate([head, shifted[8:, :]], axis=0)
        out = out + tap * w[GDN_CONV - 1 - back:GDN_CONV - back, :]
    return _silu(out)


def _l2n(x):
    return x * lax.rsqrt(jnp.sum(x * x, axis=-1, keepdims=True) + EPS)


def _tri_inverse(a):
    c = a.shape[0]
    eye = (lax.broadcasted_iota(jnp.int32, (c, c), 0)
           == lax.broadcasted_iota(jnp.int32, (c, c), 1)).astype(F32)
    hp = lax.Precision.HIGHEST
    p = eye - a
    pw = a
    span = 2
    while span < c:
        pw = jnp.dot(pw, pw, preferred_element_type=F32, precision=hp)
        p = p + jnp.dot(pw, p, preferred_element_type=F32, precision=hp)
        span *= 2
    return p


def _gdn_kernel(q_ref, k_ref, v_ref, z_ref, ba_ref, cq_ref, ck_ref, cv_ref, ga_ref, nw_ref,
                o_ref, state_ref, halo_ref, qs_ref, ks_ref, vs_ref, beta_ref, g_ref,
                *, rep, v_heads):
    hk = pl.program_id(1)
    tc = q_ref.shape[1]
    hd = HEAD_DIM

    @pl.when(pl.program_id(2) == 0)
    def _():
        state_ref[...] = jnp.zeros_like(state_ref)
        halo_ref[...] = jnp.zeros_like(halo_ref)

    q_raw = q_ref[0].astype(F32)
    k_raw = k_ref[0].astype(F32)
    v_raw = v_ref[0].astype(F32)
    q_act = _causal_conv_silu(q_raw, halo_ref[:, 0:hd], cq_ref[...])
    k_act = _causal_conv_silu(k_raw, halo_ref[:, hd:2 * hd], ck_ref[...])
    v_act = _causal_conv_silu(v_raw, halo_ref[:, 2 * hd:], cv_ref[...])
    halo_ref[:, 0:hd] = q_raw[tc - 8:tc, :]
    halo_ref[:, hd:2 * hd] = k_raw[tc - 8:tc, :]
    halo_ref[:, 2 * hd:] = v_raw[tc - 8:tc, :]
    qs_ref[...] = _l2n(q_act) * (hd ** -0.5)
    ks_ref[...] = _l2n(k_act)
    vs_ref[...] = v_act

    ba = ba_ref[0]
    beta_all = _sigmoid(ba)
    xa = ba + ga_ref[1:2, :]
    g_all = -jnp.exp(ga_ref[0:1, :]) * (jnp.maximum(xa, 0.0) + jnp.log1p(jnp.exp(-jnp.abs(xa))))
    lane = lax.broadcasted_iota(jnp.int32, ba.shape, 1)
    for r in range(rep):
        vh = hk * rep + r
        beta_col = jnp.sum(jnp.where(lane == vh, beta_all, 0.0), axis=-1, keepdims=True)
        g_col = jnp.sum(jnp.where(lane == vh + v_heads, g_all, 0.0), axis=-1, keepdims=True)
        beta_ref[r] = jnp.broadcast_to(beta_col, (tc, hd))
        g_ref[r] = jnp.broadcast_to(g_col, (tc, hd))

    c = GDN_CHUNK
    row = lax.broadcasted_iota(jnp.int32, (c, c), 0)
    col = lax.broadcasted_iota(jnp.int32, (c, c), 1)
    tril = (row >= col).astype(BF16)
    nw = nw_ref[...]

    def body(i, carry):
        sl = pl.ds(pl.multiple_of(i * c, c), c)
        q = qs_ref[sl, :]
        k = ks_ref[sl, :]
        kk = _bdot_nt(k, k)
        qk = _bdot_nt(q, k)
        for r in range(rep):
            lanes = slice(r * hd, (r + 1) * hd)
            v = vs_ref[sl, lanes]
            beta = beta_ref[r, sl, :]
            gcb = _split_dot(tril, g_ref[r, sl, :])
            gc_row = jnp.transpose(gcb)[0:c, :]
            gc_col = gcb[:, 0:c]
            g_last = gcb[c - 1:c, :]
            state = state_ref[r]

            decay = jnp.exp(jnp.where(row >= col, gc_col - gc_row, NEG))
            a = jnp.where(row > col, beta[:, 0:c] * kk * decay, 0.0)
            t_inv = _tri_inverse(a)
            u = _bdot(t_inv, v * beta)
            w = _bdot(t_inv, k * (beta * jnp.exp(gcb)))
            v_new = u - _bdot(w, state)
            o = _bdot(q * jnp.exp(gcb), state) + _bdot(qk * decay, v_new)
            kd = k * jnp.exp(g_last - gcb)
            state_ref[r] = state * jnp.exp(g_last) + _bdot_tn(kd, v_new)
            z = z_ref[0, sl, lanes].astype(F32)
            o_ref[0, sl, lanes] = (_rms(o, nw) * _silu(z)).astype(o_ref.dtype)
        return carry

    lax.fori_loop(0, tc // c, body, 0)


def gdn_recurrence(proj, ba, conv_w, a_log, dt_bias, norm_w, k_heads, v_heads):
    b, s, _ = proj.shape
    rep = v_heads // k_heads
    vw = rep * HEAD_DIM
    tc = _tile(s, 512)
    pad = HEAD_DIM - 2 * v_heads
    zeros = jnp.zeros((v_heads,), F32)
    gate_params = jnp.stack([
        jnp.concatenate([zeros, a_log.astype(F32), jnp.zeros((pad,), F32)]),
        jnp.concatenate([zeros, dt_bias.astype(F32), jnp.zeros((pad,), F32)]),
    ])
    qk_blocks = k_heads * HEAD_DIM // vw
    conv_w = conv_w.astype(F32)

    return pl.pallas_call(
        functools.partial(_gdn_kernel, rep=rep, v_heads=v_heads),
        grid=(b, k_heads, s // tc),
        in_specs=[pl.BlockSpec((1, tc, HEAD_DIM), lambda i, h, t: (i, t, h)),
                  pl.BlockSpec((1, tc, HEAD_DIM), lambda i, h, t: (i, t, k_heads + h)),
                  pl.BlockSpec((1, tc, vw), lambda i, h, t: (i, t, 2 * qk_blocks + h)),
                  pl.BlockSpec((1, tc, vw), lambda i, h, t: (i, t, 2 * qk_blocks + k_heads + h)),
                  pl.BlockSpec((1, tc, HEAD_DIM), lambda i, h, t: (i, t, 0)),
                  pl.BlockSpec((GDN_CONV, HEAD_DIM), lambda i, h, t: (0, h)),
                  pl.BlockSpec((GDN_CONV, HEAD_DIM), lambda i, h, t: (0, k_heads + h)),
                  pl.BlockSpec((GDN_CONV, vw), lambda i, h, t: (0, 2 * qk_blocks + h)),
                  pl.BlockSpec((2, HEAD_DIM), lambda i, h, t: (0, 0)),
                  pl.BlockSpec((1, HEAD_DIM), lambda i, h, t: (0, 0))],
        out_specs=pl.BlockSpec((1, tc, vw), lambda i, h, t: (i, t, h)),
        out_shape=jax.ShapeDtypeStruct((b, s, v_heads * HEAD_DIM), BF16),
        scratch_shapes=[pltpu.VMEM((rep, HEAD_DIM, HEAD_DIM), F32),
                        pltpu.VMEM((8, 2 * HEAD_DIM + vw), F32),
                        pltpu.VMEM((tc, HEAD_DIM), F32),
                        pltpu.VMEM((tc, HEAD_DIM), F32),
                        pltpu.VMEM((tc, vw), F32),
                        pltpu.VMEM((rep, tc, HEAD_DIM), F32),
                        pltpu.VMEM((rep, tc, HEAD_DIM), F32)],
        compiler_params=_cparams(("parallel", "parallel", "arbitrary")),
        name="gdn_recurrence",
    )(proj, proj, proj, proj, ba, conv_w, conv_w, conv_w, gate_params,
      norm_w.astype(F32).reshape(1, HEAD_DIM))


def kernel(x, c, ada_w, ada_b, norm_w, hg_w_in, hg_lb_logits, hg_norm_w, hg_w_out, gdn_w_in,
           gdn_conv_w, gdn_A_log, gdn_dt_bias, gdn_norm_w, gdn_w_out, ffn_w_gate_up, ffn_w_down):
    depth = ada_w.shape[0]
    b, s, d = x.shape
    v_heads = gdn_A_log.shape[1]
    k_heads = v_heads // 2
    gdn_main = 2 * k_heads * HEAD_DIM + 2 * v_heads * HEAD_DIM

    mod_all = ada_modulation(c, ada_w, ada_b).reshape(depth, b, N_MOD, d)
    for layer in range(depth):
        mod = mod_all[layer]
        nw = norm_w[layer]
        j = layer // 2
        if layer % 2 == 0:
            proj = norm_mod_project(x, nw[0:1], mod, hg_w_in[j].astype(BF16), 0, F32)
            mixed = hgrn2_recurrence(proj, hg_lb_logits, hg_norm_w[j], j)
            w_out = hg_w_out[j]
        else:
            w_in = gdn_w_in[j]
            w_tail = jnp.pad(w_in[:, gdn_main:], ((0, 0), (0, HEAD_DIM - 2 * v_heads)))
            proj = norm_mod_project(x, nw[0:1], mod, w_in[:, :gdn_main].astype(BF16), 0, BF16)
            ba = norm_mod_project(x, nw[0:1], mod, w_tail.astype(BF16), 0, F32)
            mixed = gdn_recurrence(proj, ba, gdn_conv_w[j], gdn_A_log[j], gdn_dt_bias[j],
                                   gdn_norm_w[j], k_heads, v_heads)
            w_out = gdn_w_out[j]
        x = out_project_residual(mixed, w_out.astype(BF16), x, nw[1:2], mod, 2)
        x = ffn_residual(x, nw[2:3], nw[3:4], mod, ffn_w_gate_up[layer].astype(BF16),
                         ffn_w_down[layer].astype(BF16))
    return x
```

```python
import functools

import jax
import jax.numpy as jnp
from jax import lax
from jax.experimental import pallas as pl
from jax.experimental.pallas import tpu as pltpu

F32 = jnp.float32
BF16 = jnp.bfloat16
EPS = 1e-6
N_MOD = 6
HEAD_DIM = 128
GDN_CONV = 4
GDN_CHUNK = 64
HG_CHUNK = 128
HG_LEAF = 8
NEG = -1e30
VMEM_LIMIT = 56 * 1024 * 1024


def _cparams(sem):
    return pltpu.CompilerParams(dimension_semantics=sem, vmem_limit_bytes=VMEM_LIMIT)


def _tile(n, pref):
    t = min(n, pref)
    while n % t:
        t //= 2
    return t


def _sigmoid(x):
    return 1.0 / (1.0 + jnp.exp(-x))


def _silu(x):
    return x * _sigmoid(x)


def _bdot(a, b):
    return jnp.dot(a.astype(BF16), b.astype(BF16), preferred_element_type=F32)


def _bdot_nt(a, b):
    return lax.dot_general(a.astype(BF16), b.astype(BF16), (((1,), (1,)), ((), ())),
                           preferred_element_type=F32)


def _bdot_tn(a, b):
    return lax.dot_general(a.astype(BF16), b.astype(BF16), (((0,), (0,)), ((), ())),
                           preferred_element_type=F32)


def _split_dot(a_exact_bf16, x):
    hi = x.astype(BF16)
    lo = (x - hi.astype(F32)).astype(BF16)
    return (jnp.dot(a_exact_bf16, hi, preferred_element_type=F32)
            + jnp.dot(a_exact_bf16, lo, preferred_element_type=F32))


def _rms(x, w):
    ms = jnp.mean(x * x, axis=-1, keepdims=True)
    return x * lax.rsqrt(ms + EPS) * w


def _ada_kernel(c_ref, w_ref, b_ref, o_ref):
    c = c_ref[...]
    o_ref[0] = _bdot(_silu(c), w_ref[0]) + b_ref[0]


def ada_modulation(c, ada_w, ada_b):
    depth, d, n = ada_w.shape
    b = c.shape[0]
    tn = _tile(n, 1024)
    return pl.pallas_call(
        _ada_kernel,
        grid=(depth, n // tn),
        in_specs=[pl.BlockSpec((b, d), lambda l, j: (0, 0)),
                  pl.BlockSpec((1, d, tn), lambda l, j: (l, 0, j)),
                  pl.BlockSpec((1, 1, tn), lambda l, j: (l, 0, j))],
        out_specs=pl.BlockSpec((1, b, tn), lambda l, j: (l, 0, j)),
        out_shape=jax.ShapeDtypeStruct((depth, b, n), F32),
        compiler_params=_cparams(("parallel", "parallel")),
        name="ada_modulation",
    )(c, ada_w, ada_b.reshape(depth, 1, n))


def _norm_mod(x, nw, mod_ref, shift_row):
    scale = mod_ref[0, shift_row + 1:shift_row + 2, :]
    shift = mod_ref[0, shift_row:shift_row + 1, :]
    return _rms(x, nw) * (1.0 + scale) + shift


def _proj_kernel(x_ref, nw_ref, mod_ref, w_ref, o_ref, h_ref, *, shift_row):
    @pl.when(pl.program_id(2) == 0)
    def _():
        h_ref[...] = _norm_mod(x_ref[0], nw_ref[...], mod_ref, shift_row).astype(BF16)

    o_ref[0] = jnp.dot(h_ref[...], w_ref[...], preferred_element_type=F32).astype(o_ref.dtype)


def norm_mod_project(x, nw, mod, w, shift_row, out_dtype):
    b, s, d = x.shape
    n = w.shape[1]
    tm = _tile(s, 1024)
    tn = _tile(n, 1024)
    return pl.pallas_call(
        functools.partial(_proj_kernel, shift_row=shift_row),
        grid=(b, s // tm, n // tn),
        in_specs=[pl.BlockSpec((1, tm, d), lambda i, m, j: (i, m, 0)),
                  pl.BlockSpec((1, d), lambda i, m, j: (0, 0)),
                  pl.BlockSpec((1, N_MOD, d), lambda i, m, j: (i, 0, 0)),
                  pl.BlockSpec((d, tn), lambda i, m, j: (0, j))],
        out_specs=pl.BlockSpec((1, tm, tn), lambda i, m, j: (i, m, j)),
        out_shape=jax.ShapeDtypeStruct((b, s, n), out_dtype),
        scratch_shapes=[pltpu.VMEM((tm, d), BF16)],
        compiler_params=_cparams(("parallel", "parallel", "arbitrary")),
        name="norm_mod_project",
    )(x, nw, mod, w)


def _out_kernel(a_ref, w_ref, x_ref, nw_ref, mod_ref, o_ref, *, gate_row):
    y = jnp.dot(a_ref[0], w_ref[...], preferred_element_type=F32)
    gate = mod_ref[0, gate_row:gate_row + 1, :]
    o_ref[0] = x_ref[0] + gate * _rms(y, nw_ref[...])


def out_project_residual(a, w, x, nw, mod, gate_row):
    b, s, k = a.shape
    d = w.shape[1]
    tm = _tile(s, 512)
    return pl.pallas_call(
        functools.partial(_out_kernel, gate_row=gate_row),
        grid=(b, s // tm),
        in_specs=[pl.BlockSpec((1, tm, k), lambda i, m: (i, m, 0)),
                  pl.BlockSpec((k, d), lambda i, m: (0, 0), pipeline_mode=pl.Buffered(1)),
                  pl.BlockSpec((1, tm, d), lambda i, m: (i, m, 0)),
                  pl.BlockSpec((1, d), lambda i, m: (0, 0)),
                  pl.BlockSpec((1, N_MOD, d), lambda i, m: (i, 0, 0))],
        out_specs=pl.BlockSpec((1, tm, d), lambda i, m: (i, m, 0)),
        out_shape=jax.ShapeDtypeStruct((b, s, d), F32),
        compiler_params=_cparams(("parallel", "parallel")),
        name="out_project_residual",
    )(a, w, x, nw, mod)


def _ffn_kernel(x_ref, nw_in_ref, mod_ref, wg_ref, wu_ref, wd_ref, nw_out_ref, o_ref,
                h_ref, acc_ref):
    k = pl.program_id(2)

    @pl.when(k == 0)
    def _():
        h_ref[...] = _norm_mod(x_ref[0], nw_in_ref[...], mod_ref, 3).astype(BF16)

    h = h_ref[...]
    g = jnp.dot(h, wg_ref[...], preferred_element_type=F32)
    u = jnp.dot(h, wu_ref[...], preferred_element_type=F32)
    part = jnp.dot((_silu(g) * u).astype(BF16), wd_ref[...], preferred_element_type=F32)

    @pl.when(k == 0)
    def _():
        acc_ref[...] = part

    @pl.when(k > 0)
    def _():
        acc_ref[...] += part

    @pl.when(k == pl.num_programs(2) - 1)
    def _():
        gate = mod_ref[0, 5:6, :]
        o_ref[0] = x_ref[0] + gate * _rms(acc_ref[...], nw_out_ref[...])


def ffn_residual(x, nw_in, nw_out, mod, w_gate_up, w_down):
    b, s, d = x.shape
    f = w_down.shape[0]
    tm = _tile(s, 512)
    tf = _tile(f, 512)
    nf = f // tf
    return pl.pallas_call(
        _ffn_kernel,
        grid=(b, s // tm, nf),
        in_specs=[pl.BlockSpec((1, tm, d), lambda i, m, k: (i, m, 0)),
                  pl.BlockSpec((1, d), lambda i, m, k: (0, 0)),
                  pl.BlockSpec((1, N_MOD, d), lambda i, m, k: (i, 0, 0)),
                  pl.BlockSpec((d, tf), lambda i, m, k: (0, k)),
                  pl.BlockSpec((d, tf), lambda i, m, k: (0, k + nf)),
                  pl.BlockSpec((tf, d), lambda i, m, k: (k, 0)),
                  pl.BlockSpec((1, d), lambda i, m, k: (0, 0))],
        out_specs=pl.BlockSpec((1, tm, d), lambda i, m, k: (i, m, 0)),
        out_shape=jax.ShapeDtypeStruct((b, s, d), F32),
        scratch_shapes=[pltpu.VMEM((tm, d), BF16), pltpu.VMEM((tm, d), F32)],
        compiler_params=_cparams(("parallel", "parallel", "arbitrary")),
        name="ffn_residual",
    )(x, nw_in, mod, w_gate_up, w_gate_up, w_down, nw_out)


def _row_group_bcast(x, group, row):
    c, w = x.shape
    x3 = x.reshape(c // group, group, w)
    return jnp.broadcast_to(x3[:, row:row + 1, :], x3.shape).reshape(c, w)


def _hgrn_kernel(q_ref, f_ref, v_ref, g_ref, lbl_ref, nw_ref, o_ref, state_ref, *, layer_j, hg):
    c = HG_CHUNK
    hd = HEAD_DIM
    n_chunks = q_ref.shape[1] // c
    pairs = [(j, m) for j in range(n_chunks) for m in range(hg)]

    @pl.when(pl.program_id(2) == 0)
    def _():
        state_ref[...] = jnp.zeros_like(state_ref)

    log_lb, log_1mlb = [], []
    for m in range(hg):
        logits = lbl_ref[m]
        e = jnp.exp(logits - jnp.max(logits, axis=0, keepdims=True))
        sm = e / jnp.sum(e, axis=0, keepdims=True)
        lb = jnp.sum(sm[:layer_j + 1], axis=0, keepdims=True) - sm[0:1]
        log_lb.append(jnp.log(lb))
        log_1mlb.append(jnp.log1p(-lb))

    row = lax.broadcasted_iota(jnp.int32, (c, c), 0)
    col = lax.broadcasted_iota(jnp.int32, (c, c), 1)
    rowk = lax.broadcasted_iota(jnp.int32, (c, hd), 0)
    tril = (row >= col).astype(BF16)

    def blk(ref, j, m):
        return ref[0, j * c:(j + 1) * c, m * hd:(m + 1) * hd]

    q = {(j, m): _silu(blk(q_ref, j, m)) for j, m in pairs}
    v = {(j, m): blk(v_ref, j, m) for j, m in pairs}
    log_f = {}
    for j, m in pairs:
        f_raw = blk(f_ref, j, m)
        log_sig = jnp.minimum(f_raw, 0.0) - jnp.log1p(jnp.exp(-jnp.abs(f_raw)))
        cand = log_1mlb[m] + log_sig
        log_f[j, m] = jnp.maximum(log_lb[m], cand) + jnp.log1p(jnp.exp(-jnp.abs(log_lb[m] - cand)))
    k = {jm: 1.0 - jnp.exp(log_f[jm]) for jm in pairs}
    b = {jm: _split_dot(tril, log_f[jm]) for jm in pairs}
    b_last = {jm: b[jm][c - 1:c, :] for jm in pairs}

    scores = {jm: jnp.zeros((c, c), F32) for jm in pairs}
    n = c
    while n > HG_LEAF:
        half = n // 2
        right = (rowk % n) >= half
        same = (row // n) == (col // n)
        qt, kt = {}, {}
        for jm in pairs:
            r = _row_group_bcast(b[jm], n, half - 1)
            qt[jm] = q[jm] * jnp.exp(jnp.where(right, b[jm] - r, NEG))
            kt[jm] = k[jm] * jnp.exp(jnp.where(right, NEG, r - b[jm]))
        scores = {jm: scores[jm] + jnp.where(same, _bdot_nt(qt[jm], kt[jm]), 0.0) for jm in pairs}
        n = half
    o = {jm: _bdot(scores[jm], v[jm]) for jm in pairs}

    g = c // HG_LEAF
    t_in = lax.broadcasted_iota(jnp.int32, (g, HG_LEAF, hd), 1)
    for jm in pairs:
        q3 = q[jm].reshape(g, HG_LEAF, hd)
        k3 = k[jm].reshape(g, HG_LEAF, hd)
        b3 = b[jm].reshape(g, HG_LEAF, hd)
        v3 = v[jm].reshape(g, HG_LEAF, hd)
        o3 = jnp.zeros((g, HG_LEAF, hd), F32)
        for s in range(HG_LEAF):
            p = q3 * k3[:, s:s + 1, :] * jnp.exp(jnp.where(t_in >= s, b3 - b3[:, s:s + 1, :], NEG))
            o3 = o3 + jnp.sum(p, axis=-1, keepdims=True) * v3[:, s:s + 1, :]
        o[jm] = o[jm] + o3.reshape(c, hd)

    qe = {jm: (q[jm] * jnp.exp(b[jm])).astype(BF16) for jm in pairs}
    upd = {jm: _bdot_tn(v[jm], k[jm] * jnp.exp(b_last[jm] - b[jm])) for jm in pairs}

    state = [state_ref[m] for m in range(hg)]
    for j in range(n_chunks):
        for m in range(hg):
            o[j, m] = o[j, m] + _bdot_nt(qe[j, m], state[m])
            state[m] = state[m] * jnp.exp(b_last[j, m]) + upd[j, m]
    for m in range(hg):
        state_ref[m] = state[m]

    nw = nw_ref[...]
    for j, m in pairs:
        gate = _sigmoid(blk(g_ref, j, m))
        o_ref[0, j * c:(j + 1) * c, m * hd:(m + 1) * hd] = (_rms(o[j, m], nw) * gate).astype(o_ref.dtype)


def hgrn2_recurrence(proj, lb_logits, norm_w, layer_j):
    b, s, n4 = proj.shape
    hd = n4 // 4
    heads = hd // HEAD_DIM
    hg = 2 if heads % 2 == 0 else 1
    steps = heads // hg
    tc = _tile(s, 512)
    nl = lb_logits.shape[0]
    lbl = lb_logits.astype(F32).reshape(nl, heads, HEAD_DIM).transpose(1, 0, 2)

    def col(section):
        return pl.BlockSpec((1, tc, hg * HEAD_DIM), lambda i, h, t: (i, t, section * steps + h))

    return pl.pallas_call(
        functools.partial(_hgrn_kernel, layer_j=layer_j, hg=hg),
        grid=(b, steps, s // tc),
        in_specs=[col(0), col(1), col(2), col(3),
                  pl.BlockSpec((hg, nl, HEAD_DIM), lambda i, h, t: (h, 0, 0)),
                  pl.BlockSpec((1, HEAD_DIM), lambda i, h, t: (0, 0))],
        out_specs=pl.BlockSpec((1, tc, hg * HEAD_DIM), lambda i, h, t: (i, t, h)),
        out_shape=jax.ShapeDtypeStruct((b, s, hd), BF16),
        scratch_shapes=[pltpu.VMEM((hg, HEAD_DIM, HEAD_DIM), F32)],
        compiler_params=_cparams(("parallel", "parallel", "arbitrary")),
        name="hgrn2_recurrence",
    )(proj, proj, proj, proj, lbl, norm_w.astype(F32).reshape(1, HEAD_DIM))


def _l2n(x):
    return x * lax.rsqrt(jnp.sum(x * x, axis=-1, keepdims=True) + EPS)


def _gdn_gates_kernel(ba_ref, ga_ref, o_ref, *, v_heads):
    ba = ba_ref[0]
    t = ba.shape[0]
    beta = _sigmoid(ba)
    xa = ba + ga_ref[1:2, :]
    g = -jnp.exp(ga_ref[0:1, :]) * (jnp.maximum(xa, 0.0) + jnp.log1p(jnp.exp(-jnp.abs(xa))))
    row = lax.broadcasted_iota(jnp.int32, (t, t), 0)
    col = lax.broadcasted_iota(jnp.int32, (t, t), 1)
    block_tril = jnp.where((row >= col) & (row // GDN_CHUNK == col // GDN_CHUNK), 1.0, 0.0).astype(BF16)
    gc = _split_dot(block_tril, g)
    lane = lax.broadcasted_iota(jnp.int32, ba.shape, 1)
    o_ref[0] = jnp.where(lane < v_heads, beta, gc)


def gdn_gates(ba, a_log, dt_bias, v_heads):
    b, s, w = ba.shape
    tc = _tile(s, 512)
    pad = w - 2 * v_heads
    zeros = jnp.zeros((v_heads,), F32)
    gate_params = jnp.stack([
        jnp.concatenate([zeros, a_log.astype(F32), jnp.zeros((pad,), F32)]),
        jnp.concatenate([zeros, dt_bias.astype(F32), jnp.zeros((pad,), F32)]),
    ])
    return pl.pallas_call(
        functools.partial(_gdn_gates_kernel, v_heads=v_heads),
        grid=(b, s // tc),
        in_specs=[pl.BlockSpec((1, tc, w), lambda i, t: (i, t, 0)),
                  pl.BlockSpec((2, w), lambda i, t: (0, 0))],
        out_specs=pl.BlockSpec((1, tc, w), lambda i, t: (i, t, 0)),
        out_shape=jax.ShapeDtypeStruct((b, s, w), F32),
        compiler_params=_cparams(("parallel", "parallel")),
        name="gdn_gates",
    )(ba, gate_params)


def _conv_silu(xp_ref, raw, w):
    t = raw.shape[0]
    xp_ref[0:8, :] = xp_ref[t:t + 8, :]
    xp_ref[8:, :] = raw.astype(F32)
    out = xp_ref[8:, :] * w[GDN_CONV - 1:GDN_CONV, :]
    for back in range(1, GDN_CONV):
        out = out + xp_ref[pl.ds(8 - back, t), :] * w[GDN_CONV - 1 - back:GDN_CONV - back, :]
    return _silu(out)


def _gdn_kernel(q_ref, k_ref, v_ref, z_ref, gates_ref, cq_ref, ck_ref, cv_ref, nw_ref,
                o_ref, state_ref, xq_ref, xk_ref, xv_ref, qs_ref, ks_ref, vs_ref, beta_ref, gc_ref,
                w2_ref, n_ref, qp_ref, o0_ref, *, kg, rep, v_heads):
    tc = q_ref.shape[1]
    hd = HEAD_DIM
    c = GDN_CHUNK
    n_chunks = tc // c
    nh = kg * rep

    @pl.when(pl.program_id(2) == 0)
    def _():
        state_ref[...] = jnp.zeros_like(state_ref)
        xq_ref[tc:, :] = jnp.zeros((8, xq_ref.shape[1]), F32)
        xk_ref[tc:, :] = jnp.zeros((8, xk_ref.shape[1]), F32)
        xv_ref[tc:, :] = jnp.zeros((8, xv_ref.shape[1]), F32)

    gates = gates_ref[0]
    lane = lax.broadcasted_iota(jnp.int32, gates.shape, 1)
    row = lax.broadcasted_iota(jnp.int32, (c, c), 0)
    col = lax.broadcasted_iota(jnp.int32, (c, c), 1)
    eye = (row == col).astype(F32)
    rows = [slice(j * c, (j + 1) * c) for j in range(n_chunks)]

    q_act = _conv_silu(xq_ref, q_ref[0], cq_ref[...])
    k_act = _conv_silu(xk_ref, k_ref[0], ck_ref[...])
    vs_ref[...] = _conv_silu(xv_ref, v_ref[0], cv_ref[...])
    for g in range(kg):
        lanes = slice(g * hd, (g + 1) * hd)
        qs_ref[:, lanes] = _l2n(q_act[:, lanes]) * (hd ** -0.5)
        ks_ref[:, lanes] = _l2n(k_act[:, lanes])
    for m in range(nh):
        vh = pl.program_id(1) * nh + m
        beta_col = jnp.sum(jnp.where(lane == vh, gates, 0.0), axis=-1, keepdims=True)
        gc_col = jnp.sum(jnp.where(lane == vh + v_heads, gates, 0.0), axis=-1, keepdims=True)
        beta_ref[m] = jnp.broadcast_to(beta_col, (tc, hd))
        gc_ref[m] = jnp.broadcast_to(gc_col, (tc, hd))

    pairs = [(j, m) for j in range(n_chunks) for m in range(nh)]
    kpairs = [(j, g) for j in range(n_chunks) for g in range(kg)]
    q = {(j, g): qs_ref[rows[j], g * hd:(g + 1) * hd] for j, g in kpairs}
    k = {(j, g): ks_ref[rows[j], g * hd:(g + 1) * hd] for j, g in kpairs}
    kb = {jg: k[jg].astype(BF16) for jg in kpairs}
    kk = {jg: _bdot_nt(kb[jg], kb[jg]) for jg in kpairs}
    qk = {jg: _bdot_nt(q[jg], kb[jg]) for jg in kpairs}
    beta = {(j, m): beta_ref[m, rows[j], :] for j, m in pairs}
    gcb = {(j, m): gc_ref[m, rows[j], :] for j, m in pairs}
    decay = {jm: jnp.exp(jnp.where(row >= col, gcb[jm][:, 0:c] - jnp.transpose(gcb[jm])[0:c, :], NEG))
             for jm in pairs}
    a = {(j, m): beta[j, m][:, 0:c] * kk[j, m // rep] * decay[j, m] for j, m in pairs}
    t_inv = {jm: eye - jnp.where((row // 2 == col // 2) & (row > col), a[jm], 0.0) for jm in pairs}
    node = 4
    while node <= c:
        links = (row // node == col // node) & (row % node >= node // 2) & (col % node < node // 2)
        et = {jm: _bdot(jnp.where(links, a[jm], 0.0), t_inv[jm]) for jm in pairs}
        t_inv = {jm: t_inv[jm] - _bdot(t_inv[jm], et[jm]) for jm in pairs}
        node *= 2
    tb = {jm: t_inv[jm].astype(BF16) for jm in pairs}
    e_gc = {jm: jnp.exp(gcb[jm]) for jm in pairs}
    ub = {(j, m): _bdot(tb[j, m], vs_ref[rows[j], m * hd:(m + 1) * hd] * beta[j, m]).astype(BF16)
          for j, m in pairs}
    wb = {(j, m): _bdot(tb[j, m], k[j, m // rep] * (beta[j, m] * e_gc[j, m])).astype(BF16)
          for j, m in pairs}
    attn = {(j, m): (qk[j, m // rep] * decay[j, m]).astype(BF16) for j, m in pairs}
    kdt = {(j, m): jnp.transpose(k[j, m // rep] * jnp.exp(gcb[j, m][c - 1:c, :] - gcb[j, m])).astype(BF16)
           for j, m in pairs}
    for j, m in pairs:
        w2_ref[m, j] = jnp.dot(kdt[j, m], wb[j, m], preferred_element_type=F32).astype(BF16)
    for j, m in pairs:
        n_ref[m, j] = jnp.dot(kdt[j, m], ub[j, m], preferred_element_type=F32)
    for j, m in pairs:
        qp_ref[m, rows[j], :] = (q[j, m // rep] * e_gc[j, m]
                                 - jnp.dot(attn[j, m], wb[j, m], preferred_element_type=F32)).astype(BF16)
    for j, m in pairs:
        o0_ref[m, rows[j], :] = jnp.dot(attn[j, m], ub[j, m], preferred_element_type=F32)

    def advance_chunk(i, carry):
        sl = pl.ds(pl.multiple_of(i * c, c), c)
        heads = range(nh)
        state = [state_ref[m] for m in heads]
        sb = [x.astype(BF16) for x in state]
        ws = [jnp.dot(w2_ref[m, i], sb[m], preferred_element_type=F32) for m in heads]
        o = [jnp.dot(qp_ref[m, sl, :], sb[m], preferred_element_type=F32) for m in heads]
        for m in heads:
            g_last = gc_ref[m, pl.ds(i * c + c - 1, 1), :]
            state_ref[m] = state[m] * jnp.exp(g_last) - ws[m] + n_ref[m, i]
        for m in heads:
            o0_ref[m, sl, :] += o[m]
        return carry

    lax.fori_loop(0, n_chunks, advance_chunk, 0)

    nw = nw_ref[...]
    for m in range(nh):
        lanes = slice(m * hd, (m + 1) * hd)
        z = z_ref[0, :, lanes].astype(F32)
        o_ref[0, :, lanes] = (_rms(o0_ref[m], nw) * _silu(z)).astype(o_ref.dtype)


def gdn_recurrence(proj, gates, conv_w, norm_w, k_heads, v_heads):
    b, s, _ = proj.shape
    rep = v_heads // k_heads
    kg = 2 if k_heads % 2 == 0 else 1
    nh = kg * rep
    qw = kg * HEAD_DIM
    vw = nh * HEAD_DIM
    tc = _tile(s, 512)
    c = GDN_CHUNK
    n_chunks = tc // c
    steps = k_heads // kg
    v0 = 2 * k_heads * HEAD_DIM // vw
    z0 = v0 + v_heads * HEAD_DIM // vw
    conv_w = conv_w.astype(F32)

    return pl.pallas_call(
        functools.partial(_gdn_kernel, kg=kg, rep=rep, v_heads=v_heads),
        grid=(b, steps, s // tc),
        in_specs=[pl.BlockSpec((1, tc, qw), lambda i, h, t: (i, t, h)),
                  pl.BlockSpec((1, tc, qw), lambda i, h, t: (i, t, steps + h)),
                  pl.BlockSpec((1, tc, vw), lambda i, h, t: (i, t, v0 + h)),
                  pl.BlockSpec((1, tc, vw), lambda i, h, t: (i, t, z0 + h)),
                  pl.BlockSpec((1, tc, HEAD_DIM), lambda i, h, t: (i, t, 0)),
                  pl.BlockSpec((GDN_CONV, qw), lambda i, h, t: (0, h)),
                  pl.BlockSpec((GDN_CONV, qw), lambda i, h, t: (0, steps + h)),
                  pl.BlockSpec((GDN_CONV, vw), lambda i, h, t: (0, v0 + h)),
                  pl.BlockSpec((1, HEAD_DIM), lambda i, h, t: (0, 0))],
        out_specs=pl.BlockSpec((1, tc, vw), lambda i, h, t: (i, t, h)),
        out_shape=jax.ShapeDtypeStruct((b, s, v_heads * HEAD_DIM), BF16),
        scratch_shapes=[pltpu.VMEM((nh, HEAD_DIM, HEAD_DIM), F32),
                        pltpu.VMEM((tc + 8, qw), F32),
                        pltpu.VMEM((tc + 8, qw), F32),
                        pltpu.VMEM((tc + 8, vw), F32),
                        pltpu.VMEM((tc, qw), F32),
                        pltpu.VMEM((tc, qw), F32),
                        pltpu.VMEM((tc, vw), F32),
                        pltpu.VMEM((nh, tc, HEAD_DIM), F32),
                        pltpu.VMEM((nh, tc, HEAD_DIM), F32),
                        pltpu.VMEM((nh, n_chunks, HEAD_DIM, HEAD_DIM), BF16),
                        pltpu.VMEM((nh, n_chunks, HEAD_DIM, HEAD_DIM), F32),
                        pltpu.VMEM((nh, tc, HEAD_DIM), BF16),
                        pltpu.VMEM((nh, tc, HEAD_DIM), F32)],
        compiler_params=_cparams(("parallel", "parallel", "arbitrary")),
        name="gdn_recurrence",
    )(proj, proj, proj, proj, gates, conv_w, conv_w, conv_w, norm_w.astype(F32).reshape(1, HEAD_DIM))


def kernel(x, c, ada_w, ada_b, norm_w, hg_w_in, hg_lb_logits, hg_norm_w, hg_w_out, gdn_w_in,
           gdn_conv_w, gdn_A_log, gdn_dt_bias, gdn_norm_w, gdn_w_out, ffn_w_gate_up, ffn_w_down):
    depth = ada_w.shape[0]
    b, s, d = x.shape
    v_heads = gdn_A_log.shape[1]
    k_heads = v_heads // 2
    gdn_main = 2 * k_heads * HEAD_DIM + 2 * v_heads * HEAD_DIM

    mod_all = ada_modulation(c, ada_w, ada_b).reshape(depth, b, N_MOD, d)
    for layer in range(depth):
        mod = mod_all[layer]
        nw = norm_w[layer]
        j = layer // 2
        if layer % 2 == 0:
            proj = norm_mod_project(x, nw[0:1], mod, hg_w_in[j].astype(BF16), 0, F32)
            mixed = hgrn2_recurrence(proj, hg_lb_logits, hg_norm_w[j], j)
            w_out = hg_w_out[j]
        else:
            w_in = gdn_w_in[j]
            w_tail = jnp.pad(w_in[:, gdn_main:], ((0, 0), (0, HEAD_DIM - 2 * v_heads)))
            proj = norm_mod_project(x, nw[0:1], mod, w_in[:, :gdn_main].astype(BF16), 0, BF16)
            ba = norm_mod_project(x, nw[0:1], mod, w_tail.astype(BF16), 0, F32)
            gates = gdn_gates(ba, gdn_A_log[j], gdn_dt_bias[j], v_heads)
            mixed = gdn_recurrence(proj, gates, gdn_conv_w[j], gdn_norm_w[j], k_heads, v_heads)
            w_out = gdn_w_out[j]
        x = out_project_residual(mixed, w_out.astype(BF16), x, nw[1:2], mod, 2)
        x = ffn_residual(x, nw[2:3], nw[3:4], mod, ffn_w_gate_up[layer].astype(BF16),
                         ffn_w_down[layer].astype(BF16))
    return x
```

```python
import functools

import jax
import jax.numpy as jnp
from jax import lax
from jax.experimental import pallas as pl
from jax.experimental.pallas import tpu as pltpu

F32 = jnp.float32
BF16 = jnp.bfloat16
EPS = 1e-6
N_MOD = 6
HEAD_DIM = 128
GDN_CONV = 4
GDN_CHUNK = 64
HG_CHUNK = 128
HG_LEAF = 8
NEG = -1e30
ROWS_PER_PASS = 256
VMEM_LIMIT = 56 * 1024 * 1024


def _cparams(sem):
    return pltpu.CompilerParams(dimension_semantics=sem, vmem_limit_bytes=VMEM_LIMIT)


def _tile(n, pref):
    t = min(n, pref)
    while n % t:
        t //= 2
    return t


def _sigmoid(x):
    return 1.0 / (1.0 + jnp.exp(-x))


def _silu(x):
    return x * _sigmoid(x)


def _bdot(a, b):
    return jnp.dot(a.astype(BF16), b.astype(BF16), preferred_element_type=F32)


def _bdot_nt(a, b):
    return lax.dot_general(a.astype(BF16), b.astype(BF16), (((1,), (1,)), ((), ())),
                           preferred_element_type=F32)


def _bdot_tn(a, b):
    return lax.dot_general(a.astype(BF16), b.astype(BF16), (((0,), (0,)), ((), ())),
                           preferred_element_type=F32)


def _split_dot(a_exact_bf16, x):
    hi = x.astype(BF16)
    lo = (x - hi.astype(F32)).astype(BF16)
    return (jnp.dot(a_exact_bf16, hi, preferred_element_type=F32)
            + jnp.dot(a_exact_bf16, lo, preferred_element_type=F32))


def _rms(x, w):
    ms = jnp.mean(x * x, axis=-1, keepdims=True)
    return x * lax.rsqrt(ms + EPS) * w


def _ada_kernel(c_ref, w_ref, b_ref, o_ref):
    c = c_ref[...]
    o_ref[0] = _bdot(_silu(c), w_ref[0]) + b_ref[0]


def ada_modulation(c, ada_w, ada_b):
    depth, d, n = ada_w.shape
    b = c.shape[0]
    tn = _tile(n, 1024)
    return pl.pallas_call(
        _ada_kernel,
        grid=(depth, n // tn),
        in_specs=[pl.BlockSpec((b, d), lambda l, j: (0, 0)),
                  pl.BlockSpec((1, d, tn), lambda l, j: (l, 0, j)),
                  pl.BlockSpec((1, 1, tn), lambda l, j: (l, 0, j))],
        out_specs=pl.BlockSpec((1, b, tn), lambda l, j: (l, 0, j)),
        out_shape=jax.ShapeDtypeStruct((depth, b, n), F32),
        compiler_params=_cparams(("parallel", "parallel")),
        name="ada_modulation",
    )(c, ada_w, ada_b.reshape(depth, 1, n))


def _row_passes(rows, per=ROWS_PER_PASS):
    per = _tile(rows, per)
    return [slice(r * per, (r + 1) * per) for r in range(rows // per)]


def _inv_rms(load):
    x = load()
    return lax.rsqrt(jnp.mean(x * x, axis=-1, keepdims=True) + EPS)


def _norm_mod_rows(x_ref, rows, nw, mod_ref, shift_row):
    gain = nw * (1.0 + mod_ref[0, shift_row + 1:shift_row + 2, :])
    shift = mod_ref[0, shift_row:shift_row + 1, :]
    load = lambda: x_ref[0, rows, :]
    return load() * _inv_rms(load) * gain + shift


def _proj_kernel(x_ref, nw_ref, mod_ref, w_ref, o_ref, h_ref, *, shift_row):
    @pl.when(pl.program_id(2) == 0)
    def _():
        for rows in _row_passes(h_ref.shape[0]):
            h_ref[rows, :] = _norm_mod_rows(x_ref, rows, nw_ref[...], mod_ref, shift_row).astype(BF16)

    o_ref[0] = jnp.dot(h_ref[...], w_ref[...], preferred_element_type=F32).astype(o_ref.dtype)


def norm_mod_project(x, nw, mod, w, shift_row, out_dtype, n=None):
    b, s, d = x.shape
    n = w.shape[1] if n is None else n
    tm = _tile(s, 1024)
    tn = _tile(n, 1024)
    return pl.pallas_call(
        functools.partial(_proj_kernel, shift_row=shift_row),
        grid=(b, s // tm, n // tn),
        in_specs=[pl.BlockSpec((1, tm, d), lambda i, m, j: (i, m, 0)),
                  pl.BlockSpec((1, d), lambda i, m, j: (0, 0)),
                  pl.BlockSpec((1, N_MOD, d), lambda i, m, j: (i, 0, 0)),
                  pl.BlockSpec((d, tn), lambda i, m, j: (0, j))],
        out_specs=pl.BlockSpec((1, tm, tn), lambda i, m, j: (i, m, j)),
        out_shape=jax.ShapeDtypeStruct((b, s, n), out_dtype),
        scratch_shapes=[pltpu.VMEM((tm, d), BF16)],
        compiler_params=_cparams(("parallel", "parallel", "arbitrary")),
        name="norm_mod_project",
    )(x, nw, mod, w)


def _out_kernel(a_ref, w_ref, x_ref, nw_ref, mod_ref, o_ref, *, gate_row):
    o_ref[0] = jnp.dot(a_ref[0], w_ref[...], preferred_element_type=F32)
    gain = mod_ref[0, gate_row:gate_row + 1, :] * nw_ref[...]
    for rows in _row_passes(o_ref.shape[1]):
        load = lambda: o_ref[0, rows, :]
        o_ref[0, rows, :] = x_ref[0, rows, :] + load() * _inv_rms(load) * gain


def out_project_residual(a, w, x, nw, mod, gate_row):
    b, s, k = a.shape
    d = w.shape[1]
    tm = _tile(s, 512)
    return pl.pallas_call(
        functools.partial(_out_kernel, gate_row=gate_row),
        grid=(b, s // tm),
        in_specs=[pl.BlockSpec((1, tm, k), lambda i, m: (i, m, 0)),
                  pl.BlockSpec((k, d), lambda i, m: (0, 0), pipeline_mode=pl.Buffered(1)),
                  pl.BlockSpec((1, tm, d), lambda i, m: (i, m, 0)),
                  pl.BlockSpec((1, d), lambda i, m: (0, 0)),
                  pl.BlockSpec((1, N_MOD, d), lambda i, m: (i, 0, 0))],
        out_specs=pl.BlockSpec((1, tm, d), lambda i, m: (i, m, 0)),
        out_shape=jax.ShapeDtypeStruct((b, s, d), F32),
        compiler_params=_cparams(("parallel", "parallel")),
        name="out_project_residual",
    )(a, w, x, nw, mod)


def _ffn_kernel(x_ref, nw_in_ref, mod_ref, wg_ref, wu_ref, wd_ref, nw_out_ref, o_ref, h_ref):
    k = pl.program_id(2)
    tm = h_ref.shape[0]

    @pl.when(k == 0)
    def _():
        for rows in _row_passes(tm):
            h_ref[rows, :] = _norm_mod_rows(x_ref, rows, nw_in_ref[...], mod_ref, 3).astype(BF16)
        o_ref[0] = jnp.zeros(o_ref.shape[1:], F32)

    for rows in _row_passes(tm, 512):
        h = h_ref[rows, :]
        g = jnp.dot(h, wg_ref[...], preferred_element_type=F32)
        u = jnp.dot(h, wu_ref[...], preferred_element_type=F32)
        o_ref[0, rows, :] += jnp.dot((_silu(g) * u).astype(BF16), wd_ref[...], preferred_element_type=F32)

    @pl.when(k == pl.num_programs(2) - 1)
    def _():
        gain = mod_ref[0, 5:6, :] * nw_out_ref[...]
        for rows in _row_passes(tm):
            load = lambda: o_ref[0, rows, :]
            o_ref[0, rows, :] = x_ref[0, rows, :] + load() * _inv_rms(load) * gain


def ffn_residual(x, nw_in, nw_out, mod, w_gate_up, w_down):
    b, s, d = x.shape
    f = w_down.shape[0]
    tm = _tile(s, 1024)
    tf = _tile(f, 512)
    nf = f // tf
    return pl.pallas_call(
        _ffn_kernel,
        grid=(b, s // tm, nf),
        in_specs=[pl.BlockSpec((1, tm, d), lambda i, m, k: (i, m, 0)),
                  pl.BlockSpec((1, d), lambda i, m, k: (0, 0)),
                  pl.BlockSpec((1, N_MOD, d), lambda i, m, k: (i, 0, 0)),
                  pl.BlockSpec((d, tf), lambda i, m, k: (0, k)),
                  pl.BlockSpec((d, tf), lambda i, m, k: (0, k + nf)),
                  pl.BlockSpec((tf, d), lambda i, m, k: (k, 0)),
                  pl.BlockSpec((1, d), lambda i, m, k: (0, 0))],
        out_specs=pl.BlockSpec((1, tm, d), lambda i, m, k: (i, m, 0)),
        out_shape=jax.ShapeDtypeStruct((b, s, d), F32),
        scratch_shapes=[pltpu.VMEM((tm, d), BF16)],
        compiler_params=_cparams(("parallel", "parallel", "arbitrary")),
        name="ffn_residual",
    )(x, nw_in, mod, w_gate_up, w_gate_up, w_down, nw_out)


def _row_group_bcast(x, group, row):
    c, w = x.shape
    x3 = x.reshape(c // group, group, w)
    return jnp.broadcast_to(x3[:, row:row + 1, :], x3.shape).reshape(c, w)


def _hgrn_kernel(q_ref, f_ref, v_ref, g_ref, lbl_ref, nw_ref, o_ref, state_ref, *, layer_j, hg):
    c = HG_CHUNK
    hd = HEAD_DIM
    n_chunks = q_ref.shape[1] // c
    pairs = [(j, m) for j in range(n_chunks) for m in range(hg)]

    @pl.when(pl.program_id(2) == 0)
    def _():
        state_ref[...] = jnp.zeros_like(state_ref)

    log_lb, log_1mlb = [], []
    for m in range(hg):
        logits = lbl_ref[m]
        e = jnp.exp(logits - jnp.max(logits, axis=0, keepdims=True))
        sm = e / jnp.sum(e, axis=0, keepdims=True)
        lb = jnp.sum(sm[:layer_j + 1], axis=0, keepdims=True) - sm[0:1]
        log_lb.append(jnp.log(lb))
        log_1mlb.append(jnp.log1p(-lb))

    row = lax.broadcasted_iota(jnp.int32, (c, c), 0)
    col = lax.broadcasted_iota(jnp.int32, (c, c), 1)
    rowk = lax.broadcasted_iota(jnp.int32, (c, hd), 0)
    tril = (row >= col).astype(BF16)

    def blk(ref, j, m):
        return ref[0, j * c:(j + 1) * c, m * hd:(m + 1) * hd]

    q = {(j, m): _silu(blk(q_ref, j, m)) for j, m in pairs}
    v = {(j, m): blk(v_ref, j, m) for j, m in pairs}
    log_f = {}
    for j, m in pairs:
        f_raw = blk(f_ref, j, m)
        log_sig = jnp.minimum(f_raw, 0.0) - jnp.log1p(jnp.exp(-jnp.abs(f_raw)))
        cand = log_1mlb[m] + log_sig
        log_f[j, m] = jnp.maximum(log_lb[m], cand) + jnp.log1p(jnp.exp(-jnp.abs(log_lb[m] - cand)))
    k = {jm: 1.0 - jnp.exp(log_f[jm]) for jm in pairs}
    b = {jm: _split_dot(tril, log_f[jm]) for jm in pairs}
    b_last = {jm: b[jm][c - 1:c, :] for jm in pairs}

    scores = {jm: jnp.zeros((c, c), F32) for jm in pairs}
    n = c
    while n > HG_LEAF:
        half = n // 2
        right = (rowk % n) >= half
        same = (row // n) == (col // n)
        qt, kt = {}, {}
        for jm in pairs:
            r = _row_group_bcast(b[jm], n, half - 1)
            qt[jm] = q[jm] * jnp.exp(jnp.where(right, b[jm] - r, NEG))
            kt[jm] = k[jm] * jnp.exp(jnp.where(right, NEG, r - b[jm]))
        scores = {jm: scores[jm] + jnp.where(same, _bdot_nt(qt[jm], kt[jm]), 0.0) for jm in pairs}
        n = half
    o = {jm: _bdot(scores[jm], v[jm]) for jm in pairs}

    g = c // HG_LEAF
    t_in = lax.broadcasted_iota(jnp.int32, (g, HG_LEAF, hd), 1)
    for jm in pairs:
        q3 = q[jm].reshape(g, HG_LEAF, hd)
        k3 = k[jm].reshape(g, HG_LEAF, hd)
        b3 = b[jm].reshape(g, HG_LEAF, hd)
        v3 = v[jm].reshape(g, HG_LEAF, hd)
        o3 = jnp.zeros((g, HG_LEAF, hd), F32)
        for s in range(HG_LEAF):
            p = q3 * k3[:, s:s + 1, :] * jnp.exp(jnp.where(t_in >= s, b3 - b3[:, s:s + 1, :], NEG))
            o3 = o3 + jnp.sum(p, axis=-1, keepdims=True) * v3[:, s:s + 1, :]
        o[jm] = o[jm] + o3.reshape(c, hd)

    qe = {jm: (q[jm] * jnp.exp(b[jm])).astype(BF16) for jm in pairs}
    upd = {jm: _bdot_tn(v[jm], k[jm] * jnp.exp(b_last[jm] - b[jm])) for jm in pairs}

    state = [state_ref[m] for m in range(hg)]
    for j in range(n_chunks):
        for m in range(hg):
            o[j, m] = o[j, m] + _bdot_nt(qe[j, m], state[m])
            state[m] = state[m] * jnp.exp(b_last[j, m]) + upd[j, m]
    for m in range(hg):
        state_ref[m] = state[m]

    nw = nw_ref[...]
    for j, m in pairs:
        gate = _sigmoid(blk(g_ref, j, m))
        o_ref[0, j * c:(j + 1) * c, m * hd:(m + 1) * hd] = (_rms(o[j, m], nw) * gate).astype(o_ref.dtype)


def hgrn2_recurrence(proj, lb_logits, norm_w, layer_j):
    b, s, n4 = proj.shape
    hd = n4 // 4
    heads = hd // HEAD_DIM
    hg = 2 if heads % 2 == 0 else 1
    steps = heads // hg
    tc = _tile(s, 512)
    nl = lb_logits.shape[0]
    lbl = lb_logits.astype(F32).reshape(nl, heads, HEAD_DIM).transpose(1, 0, 2)

    def col(section):
        return pl.BlockSpec((1, tc, hg * HEAD_DIM), lambda i, h, t: (i, t, section * steps + h))

    return pl.pallas_call(
        functools.partial(_hgrn_kernel, layer_j=layer_j, hg=hg),
        grid=(b, steps, s // tc),
        in_specs=[col(0), col(1), col(2), col(3),
                  pl.BlockSpec((hg, nl, HEAD_DIM), lambda i, h, t: (h, 0, 0)),
                  pl.BlockSpec((1, HEAD_DIM), lambda i, h, t: (0, 0))],
        out_specs=pl.BlockSpec((1, tc, hg * HEAD_DIM), lambda i, h, t: (i, t, h)),
        out_shape=jax.ShapeDtypeStruct((b, s, hd), BF16),
        scratch_shapes=[pltpu.VMEM((hg, HEAD_DIM, HEAD_DIM), F32)],
        compiler_params=_cparams(("parallel", "parallel", "arbitrary")),
        name="hgrn2_recurrence",
    )(proj, proj, proj, proj, lbl, norm_w.astype(F32).reshape(1, HEAD_DIM))


def _l2n(x):
    return x * lax.rsqrt(jnp.sum(x * x, axis=-1, keepdims=True) + EPS)


def _gdn_gates_kernel(ba_ref, ga_ref, o_ref, *, v_heads):
    ba = ba_ref[0]
    t = ba.shape[0]
    beta = _sigmoid(ba)
    xa = ba + ga_ref[1:2, :]
    g = -jnp.exp(ga_ref[0:1, :]) * (jnp.maximum(xa, 0.0) + jnp.log1p(jnp.exp(-jnp.abs(xa))))
    row = lax.broadcasted_iota(jnp.int32, (t, t), 0)
    col = lax.broadcasted_iota(jnp.int32, (t, t), 1)
    block_tril = jnp.where((row >= col) & (row // GDN_CHUNK == col // GDN_CHUNK), 1.0, 0.0).astype(BF16)
    gc = _split_dot(block_tril, g)
    lane = lax.broadcasted_iota(jnp.int32, ba.shape, 1)
    o_ref[0] = jnp.where(lane < v_heads, beta, gc)


def gdn_gates(ba, a_log, dt_bias, v_heads):
    b, s, w = ba.shape
    tc = _tile(s, 512)
    pad = w - 2 * v_heads
    zeros = jnp.zeros((v_heads,), F32)
    gate_params = jnp.stack([
        jnp.concatenate([zeros, a_log.astype(F32), jnp.zeros((pad,), F32)]),
        jnp.concatenate([zeros, dt_bias.astype(F32), jnp.zeros((pad,), F32)]),
    ])
    return pl.pallas_call(
        functools.partial(_gdn_gates_kernel, v_heads=v_heads),
        grid=(b, s // tc),
        in_specs=[pl.BlockSpec((1, tc, w), lambda i, t: (i, t, 0)),
                  pl.BlockSpec((2, w), lambda i, t: (0, 0))],
        out_specs=pl.BlockSpec((1, tc, w), lambda i, t: (i, t, 0)),
        out_shape=jax.ShapeDtypeStruct((b, s, w), F32),
        compiler_params=_cparams(("parallel", "parallel")),
        name="gdn_gates",
    )(ba, gate_params)


def _conv_silu(xp_ref, w, r0, nrows):
    out = xp_ref[8 + r0:8 + r0 + nrows, :] * w[GDN_CONV - 1:GDN_CONV, :]
    for back in range(1, GDN_CONV):
        out = out + xp_ref[pl.ds(8 + r0 - back, nrows), :] * w[GDN_CONV - 1 - back:GDN_CONV - back, :]
    return _silu(out)


def _gdn_kernel(q_ref, k_ref, v_ref, z_ref, gates_ref, cq_ref, ck_ref, cv_ref, nw_ref,
                o_ref, state_ref, xq_ref, xk_ref, xv_ref, qs_ref, ks_ref, vs_ref, beta_ref, gc_ref,
                w2_ref, n_ref, qp_ref, o0_ref, *, kg, rep, v_heads):
    tc = q_ref.shape[1]
    hd = HEAD_DIM
    c = GDN_CHUNK
    n_chunks = tc // c
    nh = kg * rep

    @pl.when(pl.program_id(2) == 0)
    def _():
        state_ref[...] = jnp.zeros_like(state_ref)
        xq_ref[tc:, :] = jnp.zeros((8, xq_ref.shape[1]), F32)
        xk_ref[tc:, :] = jnp.zeros((8, xk_ref.shape[1]), F32)
        xv_ref[tc:, :] = jnp.zeros((8, xv_ref.shape[1]), F32)

    row = lax.broadcasted_iota(jnp.int32, (c, c), 0)
    col = lax.broadcasted_iota(jnp.int32, (c, c), 1)
    eye = (row == col).astype(F32)
    rows = [slice(j * c, (j + 1) * c) for j in range(n_chunks)]

    for xp_ref, raw_ref in ((xq_ref, q_ref), (xk_ref, k_ref), (xv_ref, v_ref)):
        xp_ref[0:8, :] = xp_ref[tc:tc + 8, :]
        xp_ref[8:, :] = raw_ref[0].astype(F32)

    q_act = _conv_silu(xq_ref, cq_ref[...], 0, tc)
    k_act = _conv_silu(xk_ref, ck_ref[...], 0, tc)
    vs_ref[...] = _conv_silu(xv_ref, cv_ref[...], 0, tc)
    for g in range(kg):
        lanes = slice(g * hd, (g + 1) * hd)
        qs_ref[:, lanes] = _l2n(q_act[:, lanes]) * (hd ** -0.5)
        ks_ref[:, lanes] = _l2n(k_act[:, lanes])
    gates = gates_ref[0]
    lane = lax.broadcasted_iota(jnp.int32, gates.shape, 1)
    for m in range(nh):
        vh = pl.program_id(1) * nh + m
        beta_col = jnp.sum(jnp.where(lane == vh, gates, 0.0), axis=-1, keepdims=True)
        gc_col = jnp.sum(jnp.where(lane == vh + v_heads, gates, 0.0), axis=-1, keepdims=True)
        beta_ref[m] = jnp.broadcast_to(beta_col, (tc, hd))
        gc_ref[m] = jnp.broadcast_to(gc_col, (tc, hd))

    pairs = [(j, m) for j in range(n_chunks) for m in range(nh)]
    kpairs = [(j, g) for j in range(n_chunks) for g in range(kg)]
    q = {(j, g): qs_ref[rows[j], g * hd:(g + 1) * hd] for j, g in kpairs}
    k = {(j, g): ks_ref[rows[j], g * hd:(g + 1) * hd] for j, g in kpairs}
    kb = {jg: k[jg].astype(BF16) for jg in kpairs}
    kk = {jg: _bdot_nt(kb[jg], kb[jg]) for jg in kpairs}
    qk = {jg: _bdot_nt(q[jg], kb[jg]) for jg in kpairs}
    beta = {(j, m): beta_ref[m, rows[j], :] for j, m in pairs}
    gcb = {(j, m): gc_ref[m, rows[j], :] for j, m in pairs}
    decay = {jm: jnp.exp(jnp.where(row >= col, gcb[jm][:, 0:c] - jnp.transpose(gcb[jm])[0:c, :], NEG))
             for jm in pairs}
    a = {(j, m): beta[j, m][:, 0:c] * kk[j, m // rep] * decay[j, m] for j, m in pairs}
    t_inv = {jm: eye - jnp.where((row // 2 == col // 2) & (row > col), a[jm], 0.0) for jm in pairs}
    node = 4
    while node <= c:
        links = (row // node == col // node) & (row % node >= node // 2) & (col % node < node // 2)
        et = {jm: _bdot(jnp.where(links, a[jm], 0.0), t_inv[jm]) for jm in pairs}
        t_inv = {jm: t_inv[jm] - _bdot(t_inv[jm], et[jm]) for jm in pairs}
        node *= 2
    tb = {jm: t_inv[jm].astype(BF16) for jm in pairs}
    e_gc = {jm: jnp.exp(gcb[jm]) for jm in pairs}
    ub = {(j, m): _bdot(tb[j, m], vs_ref[rows[j], m * hd:(m + 1) * hd] * beta[j, m]).astype(BF16)
          for j, m in pairs}
    wb = {(j, m): _bdot(tb[j, m], k[j, m // rep] * (beta[j, m] * e_gc[j, m])).astype(BF16)
          for j, m in pairs}
    attn = {(j, m): (qk[j, m // rep] * decay[j, m]).astype(BF16) for j, m in pairs}
    kdt = {(j, m): jnp.transpose(k[j, m // rep] * jnp.exp(gcb[j, m][c - 1:c, :] - gcb[j, m])).astype(BF16)
           for j, m in pairs}
    for j, m in pairs:
        w2_ref[m, j] = jnp.dot(kdt[j, m], wb[j, m], preferred_element_type=F32).astype(BF16)
    for j, m in pairs:
        n_ref[m, j] = jnp.dot(kdt[j, m], ub[j, m], preferred_element_type=F32)
    for j, m in pairs:
        qp_ref[m, rows[j], :] = (q[j, m // rep] * e_gc[j, m]
                                 - jnp.dot(attn[j, m], wb[j, m], preferred_element_type=F32)).astype(BF16)
    for j, m in pairs:
        o0_ref[m, rows[j], :] = jnp.dot(attn[j, m], ub[j, m], preferred_element_type=F32)

    heads = range(nh)
    nw = nw_ref[...]

    def advance(i):
        sl = pl.ds(pl.multiple_of(i * c, c), c)
        state = [state_ref[m] for m in heads]
        sb = [x.astype(BF16) for x in state]
        ws = [jnp.dot(w2_ref[m, i], sb[m], preferred_element_type=F32) for m in heads]
        o = [jnp.dot(qp_ref[m, sl, :], sb[m], preferred_element_type=F32) for m in heads]
        for m in heads:
            g_last = gc_ref[m, pl.ds(i * c + c - 1, 1), :]
            state_ref[m] = state[m] * jnp.exp(g_last) - ws[m] + n_ref[m, i]
        for m in heads:
            o0_ref[m, sl, :] += o[m]

    def finish(i):
        sl = pl.ds(pl.multiple_of(i * c, c), c)
        for m in heads:
            lanes = slice(m * hd, (m + 1) * hd)
            z = z_ref[0, sl, lanes].astype(F32)
            o_ref[0, sl, lanes] = (_rms(o0_ref[m, sl, :], nw) * _silu(z)).astype(o_ref.dtype)

    advance(0)

    def body(i, carry):
        finish(i - 1)
        advance(i)
        return carry

    lax.fori_loop(1, n_chunks, body, 0)
    finish(n_chunks - 1)


def gdn_recurrence(proj, gates, conv_w, norm_w, k_heads, v_heads):
    b, s, _ = proj.shape
    rep = v_heads // k_heads
    kg = 4 if k_heads % 4 == 0 else 1
    nh = kg * rep
    qw = kg * HEAD_DIM
    vw = nh * HEAD_DIM
    tc = _tile(s, 512)
    c = GDN_CHUNK
    n_chunks = tc // c
    steps = k_heads // kg
    v0 = 2 * k_heads * HEAD_DIM // vw
    z0 = v0 + v_heads * HEAD_DIM // vw
    conv_w = conv_w.astype(F32)

    return pl.pallas_call(
        functools.partial(_gdn_kernel, kg=kg, rep=rep, v_heads=v_heads),
        grid=(b, steps, s // tc),
        in_specs=[pl.BlockSpec((1, tc, qw), lambda i, h, t: (i, t, h)),
                  pl.BlockSpec((1, tc, qw), lambda i, h, t: (i, t, steps + h)),
                  pl.BlockSpec((1, tc, vw), lambda i, h, t: (i, t, v0 + h)),
                  pl.BlockSpec((1, tc, vw), lambda i, h, t: (i, t, z0 + h)),
                  pl.BlockSpec((1, tc, HEAD_DIM), lambda i, h, t: (i, t, 0)),
                  pl.BlockSpec((GDN_CONV, qw), lambda i, h, t: (0, h)),
                  pl.BlockSpec((GDN_CONV, qw), lambda i, h, t: (0, steps + h)),
                  pl.BlockSpec((GDN_CONV, vw), lambda i, h, t: (0, v0 + h)),
                  pl.BlockSpec((1, HEAD_DIM), lambda i, h, t: (0, 0))],
        out_specs=pl.BlockSpec((1, tc, vw), lambda i, h, t: (i, t, h)),
        out_shape=jax.ShapeDtypeStruct((b, s, v_heads * HEAD_DIM), BF16),
        scratch_shapes=[pltpu.VMEM((nh, HEAD_DIM, HEAD_DIM), F32),
                        pltpu.VMEM((tc + 8, qw), F32),
                        pltpu.VMEM((tc + 8, qw), F32),
                        pltpu.VMEM((tc + 8, vw), F32),
                        pltpu.VMEM((tc, qw), F32),
                        pltpu.VMEM((tc, qw), F32),
                        pltpu.VMEM((tc, vw), F32),
                        pltpu.VMEM((nh, tc, HEAD_DIM), F32),
                        pltpu.VMEM((nh, tc, HEAD_DIM), F32),
                        pltpu.VMEM((nh, n_chunks, HEAD_DIM, HEAD_DIM), BF16),
                        pltpu.VMEM((nh, n_chunks, HEAD_DIM, HEAD_DIM), F32),
                        pltpu.VMEM((nh, tc, HEAD_DIM), BF16),
                        pltpu.VMEM((nh, tc, HEAD_DIM), F32)],
        compiler_params=_cparams(("parallel", "parallel", "arbitrary")),
        name="gdn_recurrence",
    )(proj, proj, proj, proj, gates, conv_w, conv_w, conv_w, norm_w.astype(F32).reshape(1, HEAD_DIM))


def kernel(x, c, ada_w, ada_b, norm_w, hg_w_in, hg_lb_logits, hg_norm_w, hg_w_out, gdn_w_in,
           gdn_conv_w, gdn_A_log, gdn_dt_bias, gdn_norm_w, gdn_w_out, ffn_w_gate_up, ffn_w_down):
    depth = ada_w.shape[0]
    b, s, d = x.shape
    v_heads = gdn_A_log.shape[1]
    k_heads = v_heads // 2
    gdn_main = 2 * k_heads * HEAD_DIM + 2 * v_heads * HEAD_DIM

    mod_all = ada_modulation(c, ada_w, ada_b).reshape(depth, b, N_MOD, d)
    for layer in range(depth):
        mod = mod_all[layer]
        nw = norm_w[layer]
        j = layer // 2
        if layer % 2 == 0:
            proj = norm_mod_project(x, nw[0:1], mod, hg_w_in[j].astype(BF16), 0, F32)
            mixed = hgrn2_recurrence(proj, hg_lb_logits, hg_norm_w[j], j)
            w_out = hg_w_out[j]
        else:
            w_in = gdn_w_in[j]
            w_tail = jnp.pad(w_in[:, gdn_main:], ((0, 0), (0, HEAD_DIM - 2 * v_heads)))
            proj = norm_mod_project(x, nw[0:1], mod, w_in.astype(BF16), 0, BF16, n=gdn_main)
            ba = norm_mod_project(x, nw[0:1], mod, w_tail.astype(BF16), 0, F32)
            gates = gdn_gates(ba, gdn_A_log[j], gdn_dt_bias[j], v_heads)
            mixed = gdn_recurrence(proj, gates, gdn_conv_w[j], gdn_norm_w[j], k_heads, v_heads)
            w_out = gdn_w_out[j]
        x = out_project_residual(mixed, w_out.astype(BF16), x, nw[1:2], mod, 2)
        x = ffn_residual(x, nw[2:3], nw[3:4], mod, ffn_w_gate_up[layer].astype(BF16),
                         ffn_w_down[layer].astype(BF16))
    return x
```

```python
import functools

import jax
import jax.numpy as jnp
from jax import lax
from jax.experimental import pallas as pl
from jax.experimental.pallas import tpu as pltpu

F32 = jnp.float32
BF16 = jnp.bfloat16
EPS = 1e-6
N_MOD = 6
HEAD_DIM = 128
GDN_CONV = 4
GDN_CHUNK = 64
HG_CHUNK = 128
HG_LEAF = 8
NEG = -1e30
LOG2E = 1.4426950408889634
ROWS_PER_PASS = 256
VMEM_LIMIT = 56 * 1024 * 1024


def _cparams(sem):
    return pltpu.CompilerParams(dimension_semantics=sem, vmem_limit_bytes=VMEM_LIMIT)


def _tile(n, pref):
    t = min(n, pref)
    while n % t:
        t //= 2
    return t


def _sigmoid(x):
    return 1.0 / (1.0 + jnp.exp(-x))


def _silu(x):
    return x * _sigmoid(x)


def _bdot(a, b):
    return jnp.dot(a.astype(BF16), b.astype(BF16), preferred_element_type=F32)


def _bdot_nt(a, b):
    return lax.dot_general(a.astype(BF16), b.astype(BF16), (((1,), (1,)), ((), ())),
                           preferred_element_type=F32)


def _bdot_tn(a, b):
    return lax.dot_general(a.astype(BF16), b.astype(BF16), (((0,), (0,)), ((), ())),
                           preferred_element_type=F32)


def _split_dot(a_exact_bf16, x):
    hi = x.astype(BF16)
    lo = (x - hi.astype(F32)).astype(BF16)
    return (jnp.dot(a_exact_bf16, hi, preferred_element_type=F32)
            + jnp.dot(a_exact_bf16, lo, preferred_element_type=F32))


def _rms(x, w):
    ms = jnp.mean(x * x, axis=-1, keepdims=True)
    return x * lax.rsqrt(ms + EPS) * w


def _ada_kernel(c_ref, w_ref, b_ref, o_ref):
    c = c_ref[...]
    o_ref[0] = _bdot(_silu(c), w_ref[0]) + b_ref[0]


def ada_modulation(c, ada_w, ada_b):
    depth, d, n = ada_w.shape
    b = c.shape[0]
    tn = _tile(n, 1024)
    return pl.pallas_call(
        _ada_kernel,
        grid=(depth, n // tn),
        in_specs=[pl.BlockSpec((b, d), lambda l, j: (0, 0)),
                  pl.BlockSpec((1, d, tn), lambda l, j: (l, 0, j)),
                  pl.BlockSpec((1, 1, tn), lambda l, j: (l, 0, j))],
        out_specs=pl.BlockSpec((1, b, tn), lambda l, j: (l, 0, j)),
        out_shape=jax.ShapeDtypeStruct((depth, b, n), F32),
        compiler_params=_cparams(("parallel", "parallel")),
        name="ada_modulation",
    )(c, ada_w, ada_b.reshape(depth, 1, n))


def _row_passes(rows, per=ROWS_PER_PASS):
    per = _tile(rows, per)
    return [slice(r * per, (r + 1) * per) for r in range(rows // per)]


def _inv_rms(load):
    x = load()
    return lax.rsqrt(jnp.mean(x * x, axis=-1, keepdims=True) + EPS)


def _norm_mod_rows(x_ref, rows, nw, mod_ref, shift_row):
    gain = nw * (1.0 + mod_ref[0, shift_row + 1:shift_row + 2, :])
    shift = mod_ref[0, shift_row:shift_row + 1, :]
    load = lambda: x_ref[0, rows, :]
    return load() * _inv_rms(load) * gain + shift


def _proj_kernel(x_ref, nw_ref, mod_ref, w_ref, o_ref, h_ref, *, shift_row):
    @pl.when(pl.program_id(2) == 0)
    def _():
        for rows in _row_passes(h_ref.shape[0]):
            h_ref[rows, :] = _norm_mod_rows(x_ref, rows, nw_ref[...], mod_ref, shift_row).astype(BF16)

    o_ref[0] = jnp.dot(h_ref[...], w_ref[...], preferred_element_type=F32).astype(o_ref.dtype)


def norm_mod_project(x, nw, mod, w, shift_row, out_dtype, n=None):
    b, s, d = x.shape
    n = w.shape[1] if n is None else n
    tm = _tile(s, 1024)
    tn = _tile(n, 1024)
    return pl.pallas_call(
        functools.partial(_proj_kernel, shift_row=shift_row),
        grid=(b, s // tm, n // tn),
        in_specs=[pl.BlockSpec((1, tm, d), lambda i, m, j: (i, m, 0)),
                  pl.BlockSpec((1, d), lambda i, m, j: (0, 0)),
                  pl.BlockSpec((1, N_MOD, d), lambda i, m, j: (i, 0, 0)),
                  pl.BlockSpec((d, tn), lambda i, m, j: (0, j))],
        out_specs=pl.BlockSpec((1, tm, tn), lambda i, m, j: (i, m, j)),
        out_shape=jax.ShapeDtypeStruct((b, s, n), out_dtype),
        scratch_shapes=[pltpu.VMEM((tm, d), BF16)],
        compiler_params=_cparams(("parallel", "parallel", "arbitrary")),
        name="norm_mod_project",
    )(x, nw, mod, w)


def _out_kernel(a_ref, w_ref, x_ref, nw_ref, mod_ref, o_ref, *, gate_row):
    o_ref[0] = jnp.dot(a_ref[0], w_ref[...], preferred_element_type=F32)
    gain = mod_ref[0, gate_row:gate_row + 1, :] * nw_ref[...]
    for rows in _row_passes(o_ref.shape[1]):
        load = lambda: o_ref[0, rows, :]
        o_ref[0, rows, :] = x_ref[0, rows, :] + load() * _inv_rms(load) * gain


def out_project_residual(a, w, x, nw, mod, gate_row):
    b, s, k = a.shape
    d = w.shape[1]
    tm = _tile(s, 512)
    return pl.pallas_call(
        functools.partial(_out_kernel, gate_row=gate_row),
        grid=(b, s // tm),
        in_specs=[pl.BlockSpec((1, tm, k), lambda i, m: (i, m, 0)),
                  pl.BlockSpec((k, d), lambda i, m: (0, 0), pipeline_mode=pl.Buffered(1)),
                  pl.BlockSpec((1, tm, d), lambda i, m: (i, m, 0)),
                  pl.BlockSpec((1, d), lambda i, m: (0, 0)),
                  pl.BlockSpec((1, N_MOD, d), lambda i, m: (i, 0, 0))],
        out_specs=pl.BlockSpec((1, tm, d), lambda i, m: (i, m, 0)),
        out_shape=jax.ShapeDtypeStruct((b, s, d), F32),
        compiler_params=_cparams(("parallel", "parallel")),
        name="out_project_residual",
    )(a, w, x, nw, mod)


def _ffn_kernel(x_ref, nw_in_ref, mod_ref, wg_ref, wu_ref, wd_ref, nw_out_ref, o_ref, h_ref):
    k = pl.program_id(2)
    tm = h_ref.shape[0]

    @pl.when(k == 0)
    def _():
        for rows in _row_passes(tm):
            h_ref[rows, :] = _norm_mod_rows(x_ref, rows, nw_in_ref[...], mod_ref, 3).astype(BF16)
        o_ref[0] = jnp.zeros(o_ref.shape[1:], F32)

    for rows in _row_passes(tm, 512):
        h = h_ref[rows, :]
        g = jnp.dot(h, wg_ref[...], preferred_element_type=F32)
        u = jnp.dot(h, wu_ref[...], preferred_element_type=F32)
        o_ref[0, rows, :] += jnp.dot((_silu(g) * u).astype(BF16), wd_ref[...], preferred_element_type=F32)

    @pl.when(k == pl.num_programs(2) - 1)
    def _():
        gain = mod_ref[0, 5:6, :] * nw_out_ref[...]
        for rows in _row_passes(tm):
            load = lambda: o_ref[0, rows, :]
            o_ref[0, rows, :] = x_ref[0, rows, :] + load() * _inv_rms(load) * gain


def ffn_residual(x, nw_in, nw_out, mod, w_gate_up, w_down):
    b, s, d = x.shape
    f = w_down.shape[0]
    tm = _tile(s, 1024)
    tf = _tile(f, 512)
    nf = f // tf
    return pl.pallas_call(
        _ffn_kernel,
        grid=(b, s // tm, nf),
        in_specs=[pl.BlockSpec((1, tm, d), lambda i, m, k: (i, m, 0)),
                  pl.BlockSpec((1, d), lambda i, m, k: (0, 0)),
                  pl.BlockSpec((1, N_MOD, d), lambda i, m, k: (i, 0, 0)),
                  pl.BlockSpec((d, tf), lambda i, m, k: (0, k)),
                  pl.BlockSpec((d, tf), lambda i, m, k: (0, k + nf)),
                  pl.BlockSpec((tf, d), lambda i, m, k: (k, 0)),
                  pl.BlockSpec((1, d), lambda i, m, k: (0, 0))],
        out_specs=pl.BlockSpec((1, tm, d), lambda i, m, k: (i, m, 0)),
        out_shape=jax.ShapeDtypeStruct((b, s, d), F32),
        scratch_shapes=[pltpu.VMEM((tm, d), BF16)],
        compiler_params=_cparams(("parallel", "parallel", "arbitrary")),
        name="ffn_residual",
    )(x, nw_in, mod, w_gate_up, w_gate_up, w_down, nw_out)


def _row_group_bcast(x, group, row):
    c, w = x.shape
    x3 = x.reshape(c // group, group, w)
    return jnp.broadcast_to(x3[:, row:row + 1, :], x3.shape).reshape(c, w)


def _hgrn_kernel(q_ref, f_ref, v_ref, g_ref, lbl_ref, nw_ref, o_ref, state_ref, kleaf_ref, bleaf_ref,
                 *, layer_j, hg):
    c = HG_CHUNK
    hd = HEAD_DIM
    n_chunks = q_ref.shape[1] // c
    pairs = [(j, m) for j in range(n_chunks) for m in range(hg)]

    @pl.when(pl.program_id(2) == 0)
    def _():
        state_ref[...] = jnp.zeros_like(state_ref)

    log_lb, log_1mlb = [], []
    for m in range(hg):
        logits = lbl_ref[m]
        e = jnp.exp(logits - jnp.max(logits, axis=0, keepdims=True))
        sm = e / jnp.sum(e, axis=0, keepdims=True)
        lb = jnp.sum(sm[:layer_j + 1], axis=0, keepdims=True) - sm[0:1]
        log_lb.append(jnp.log(lb))
        log_1mlb.append(jnp.log1p(-lb))

    row = lax.broadcasted_iota(jnp.int32, (c, c), 0)
    col = lax.broadcasted_iota(jnp.int32, (c, c), 1)
    rowk = lax.broadcasted_iota(jnp.int32, (c, hd), 0)
    tril = (row >= col).astype(BF16)

    def blk(ref, j, m):
        return ref[0, j * c:(j + 1) * c, m * hd:(m + 1) * hd]

    q = {(j, m): _silu(blk(q_ref, j, m)) for j, m in pairs}
    v = {(j, m): blk(v_ref, j, m) for j, m in pairs}
    log_f = {}
    for j, m in pairs:
        f_raw = blk(f_ref, j, m)
        log_sig = jnp.minimum(f_raw, 0.0) - jnp.log(1.0 + jnp.exp(-jnp.abs(f_raw)))
        cand = log_1mlb[m] + log_sig
        log_f[j, m] = jnp.maximum(log_lb[m], cand) + jnp.log(1.0 + jnp.exp(-jnp.abs(log_lb[m] - cand)))
    k = {jm: 1.0 - jnp.exp(log_f[jm]) for jm in pairs}
    b = {jm: _split_dot(tril, log_f[jm]) * LOG2E for jm in pairs}
    b_last = {jm: b[jm][c - 1:c, :] for jm in pairs}

    scores = {jm: jnp.zeros((c, c), F32) for jm in pairs}
    n = c
    while n > HG_LEAF:
        half = n // 2
        right = (rowk % n) >= half
        same = (row // n) == (col // n)
        qt, kt = {}, {}
        for jm in pairs:
            r = _row_group_bcast(b[jm], n, half - 1)
            qt[jm] = q[jm] * jnp.exp2(jnp.where(right, b[jm] - r, NEG))
            kt[jm] = k[jm] * jnp.exp2(jnp.where(right, NEG, r - b[jm]))
        scores = {jm: scores[jm] + jnp.where(same, _bdot_nt(qt[jm], kt[jm]), 0.0) for jm in pairs}
        n = half

    g = c // HG_LEAF
    t_in = lax.broadcasted_iota(jnp.int32, (g, HG_LEAF, hd), 1)
    col3 = lax.broadcasted_iota(jnp.int32, (g, HG_LEAF, c), 2)
    blk0 = lax.broadcasted_iota(jnp.int32, (g, HG_LEAF, c), 0) * HG_LEAF
    for p_idx, jm in enumerate(pairs):
        q3 = q[jm].reshape(g, HG_LEAF, hd)
        b3 = b[jm].reshape(g, HG_LEAF, hd)
        s3 = scores[jm].reshape(g, HG_LEAF, c)
        kleaf_ref[p_idx] = k[jm].reshape(g, HG_LEAF, hd)
        bleaf_ref[p_idx] = b3
        for s in range(HG_LEAF):
            ks = kleaf_ref[p_idx, :, pl.ds(s, 1), :]
            bs = bleaf_ref[p_idx, :, pl.ds(s, 1), :]
            p = q3 * ks * jnp.exp2(jnp.where(t_in >= s, b3 - bs, NEG))
            s3 = jnp.where(col3 == blk0 + s, jnp.sum(p, axis=-1, keepdims=True), s3)
        scores[jm] = s3.reshape(c, c)
    o = {jm: _bdot(scores[jm], v[jm]) for jm in pairs}

    qe = {jm: (q[jm] * jnp.exp2(b[jm])).astype(BF16) for jm in pairs}
    upd = {jm: _bdot_tn(v[jm], k[jm] * jnp.exp2(b_last[jm] - b[jm])) for jm in pairs}

    state = [state_ref[m] for m in range(hg)]
    for j in range(n_chunks):
        for m in range(hg):
            o[j, m] = o[j, m] + _bdot_nt(qe[j, m], state[m])
            state[m] = state[m] * jnp.exp2(b_last[j, m]) + upd[j, m]
    for m in range(hg):
        state_ref[m] = state[m]

    nw = nw_ref[...]
    for j, m in pairs:
        gate = _sigmoid(blk(g_ref, j, m))
        o_ref[0, j * c:(j + 1) * c, m * hd:(m + 1) * hd] = (_rms(o[j, m], nw) * gate).astype(o_ref.dtype)


def hgrn2_recurrence(proj, lb_logits, norm_w, layer_j):
    b, s, n4 = proj.shape
    hd = n4 // 4
    heads = hd // HEAD_DIM
    hg = 2 if heads % 2 == 0 else 1
    steps = heads // hg
    tc = _tile(s, 512)
    nl = lb_logits.shape[0]
    lbl = lb_logits.astype(F32).reshape(nl, heads, HEAD_DIM).transpose(1, 0, 2)
    leaf_shape = (hg * (tc // HG_CHUNK), HG_CHUNK // HG_LEAF, HG_LEAF, HEAD_DIM)

    def col(section):
        return pl.BlockSpec((1, tc, hg * HEAD_DIM), lambda i, h, t: (i, t, section * steps + h))

    return pl.pallas_call(
        functools.partial(_hgrn_kernel, layer_j=layer_j, hg=hg),
        grid=(b, steps, s // tc),
        in_specs=[col(0), col(1), col(2), col(3),
                  pl.BlockSpec((hg, nl, HEAD_DIM), lambda i, h, t: (h, 0, 0)),
                  pl.BlockSpec((1, HEAD_DIM), lambda i, h, t: (0, 0))],
        out_specs=pl.BlockSpec((1, tc, hg * HEAD_DIM), lambda i, h, t: (i, t, h)),
        out_shape=jax.ShapeDtypeStruct((b, s, hd), BF16),
        scratch_shapes=[pltpu.VMEM((hg, HEAD_DIM, HEAD_DIM), F32),
                        pltpu.VMEM(leaf_shape, F32),
                        pltpu.VMEM(leaf_shape, F32)],
        compiler_params=_cparams(("parallel", "parallel", "arbitrary")),
        name="hgrn2_recurrence",
    )(proj, proj, proj, proj, lbl, norm_w.astype(F32).reshape(1, HEAD_DIM))


def _l2n(x):
    return x * lax.rsqrt(jnp.sum(x * x, axis=-1, keepdims=True) + EPS)


def _gdn_gates_kernel(ba_ref, ga_ref, o_ref, *, v_heads):
    ba = ba_ref[0]
    t = ba.shape[0]
    beta = _sigmoid(ba)
    xa = ba + ga_ref[1:2, :]
    g = -jnp.exp(ga_ref[0:1, :]) * (jnp.maximum(xa, 0.0) + jnp.log1p(jnp.exp(-jnp.abs(xa))))
    row = lax.broadcasted_iota(jnp.int32, (t, t), 0)
    col = lax.broadcasted_iota(jnp.int32, (t, t), 1)
    block_tril = jnp.where((row >= col) & (row // GDN_CHUNK == col // GDN_CHUNK), 1.0, 0.0).astype(BF16)
    gc = _split_dot(block_tril, g)
    lane = lax.broadcasted_iota(jnp.int32, ba.shape, 1)
    o_ref[0] = jnp.where(lane < v_heads, beta, gc)


def gdn_gates(ba, a_log, dt_bias, v_heads):
    b, s, w = ba.shape
    tc = _tile(s, 512)
    pad = w - 2 * v_heads
    zeros = jnp.zeros((v_heads,), F32)
    gate_params = jnp.stack([
        jnp.concatenate([zeros, a_log.astype(F32), jnp.zeros((pad,), F32)]),
        jnp.concatenate([zeros, dt_bias.astype(F32), jnp.zeros((pad,), F32)]),
    ])
    return pl.pallas_call(
        functools.partial(_gdn_gates_kernel, v_heads=v_heads),
        grid=(b, s // tc),
        in_specs=[pl.BlockSpec((1, tc, w), lambda i, t: (i, t, 0)),
                  pl.BlockSpec((2, w), lambda i, t: (0, 0))],
        out_specs=pl.BlockSpec((1, tc, w), lambda i, t: (i, t, 0)),
        out_shape=jax.ShapeDtypeStruct((b, s, w), F32),
        compiler_params=_cparams(("parallel", "parallel")),
        name="gdn_gates",
    )(ba, gate_params)


def _conv_silu(xp_ref, w, lanes, nrows):
    out = xp_ref[8:8 + nrows, lanes] * w[GDN_CONV - 1:GDN_CONV, :]
    for back in range(1, GDN_CONV):
        out = out + xp_ref[pl.ds(8 - back, nrows), lanes] * w[GDN_CONV - 1 - back:GDN_CONV - back, :]
    return _silu(out)


def _gdn_kernel(q_ref, k_ref, v_ref, z_ref, gates_ref, cq_ref, ck_ref, cv_ref, nw_ref,
                o_ref, state_ref, xq_ref, xk_ref, xv_ref, gc_ref,
                w2_ref, n_ref, qp_ref, o0_ref, *, kg, rep, v_heads):
    tc = q_ref.shape[1]
    hd = HEAD_DIM
    c = GDN_CHUNK
    n_chunks = tc // c
    nh = kg * rep

    @pl.when(pl.program_id(2) == 0)
    def _():
        state_ref[...] = jnp.zeros_like(state_ref)
        xq_ref[tc:, :] = jnp.zeros((8, xq_ref.shape[1]), F32)
        xk_ref[tc:, :] = jnp.zeros((8, xk_ref.shape[1]), F32)
        xv_ref[tc:, :] = jnp.zeros((8, xv_ref.shape[1]), F32)

    row = lax.broadcasted_iota(jnp.int32, (c, c), 0)
    col = lax.broadcasted_iota(jnp.int32, (c, c), 1)
    eye = (row == col).astype(F32)
    rows = [slice(j * c, (j + 1) * c) for j in range(n_chunks)]

    for xp_ref, raw_ref in ((xq_ref, q_ref), (xk_ref, k_ref), (xv_ref, v_ref)):
        xp_ref[0:8, :] = xp_ref[tc:tc + 8, :]
        xp_ref[8:, :] = raw_ref[0].astype(F32)

    def prepare(khs):
        q, k, v, beta, gcb = {}, {}, {}, {}, {}
        gates = gates_ref[0]
        lane = lax.broadcasted_iota(jnp.int32, gates.shape, 1)
        for g in khs:
            lanes = slice(g * hd, (g + 1) * hd)
            qn = _l2n(_conv_silu(xq_ref, cq_ref[:, lanes], lanes, tc)) * (hd ** -0.5)
            kn = _l2n(_conv_silu(xk_ref, ck_ref[:, lanes], lanes, tc))
            for j in range(n_chunks):
                q[j, g] = qn[rows[j], :]
                k[j, g] = kn[rows[j], :]
            for m in range(g * rep, (g + 1) * rep):
                vlanes = slice(m * hd, (m + 1) * hd)
                va = _conv_silu(xv_ref, cv_ref[:, vlanes], vlanes, tc)
                vh = pl.program_id(1) * nh + m
                beta_col = jnp.sum(jnp.where(lane == vh, gates, 0.0), axis=-1, keepdims=True)
                gc_col = jnp.sum(jnp.where(lane == vh + v_heads, gates, 0.0), axis=-1, keepdims=True)
                beta_b = jnp.broadcast_to(beta_col, (tc, hd))
                gc_b = jnp.broadcast_to(gc_col, (tc, hd))
                gc_ref[m] = gc_b
                for j in range(n_chunks):
                    v[j, m] = va[rows[j], :]
                    beta[j, m] = beta_b[rows[j], :]
                    gcb[j, m] = gc_b[rows[j], :]
        return q, k, v, beta, gcb

    def solve(khs, prepared):
        q, k, v, beta, gcb = prepared
        pairs = [(j, m) for j in range(n_chunks) for g in khs for m in range(g * rep, (g + 1) * rep)]
        kpairs = [(j, g) for j in range(n_chunks) for g in khs]
        kb = {jg: k[jg].astype(BF16) for jg in kpairs}
        kk = {jg: _bdot_nt(kb[jg], kb[jg]) for jg in kpairs}
        qk = {jg: _bdot_nt(q[jg], kb[jg]) for jg in kpairs}
        decay = {jm: jnp.exp(jnp.where(row >= col, gcb[jm][:, 0:c] - jnp.transpose(gcb[jm])[0:c, :], NEG))
                 for jm in pairs}
        a = {(j, m): beta[j, m][:, 0:c] * kk[j, m // rep] * decay[j, m] for j, m in pairs}
        t_inv = {jm: eye - jnp.where((row // 2 == col // 2) & (row > col), a[jm], 0.0) for jm in pairs}
        node = 4
        while node <= c:
            links = (row // node == col // node) & (row % node >= node // 2) & (col % node < node // 2)
            et = {jm: _bdot(jnp.where(links, a[jm], 0.0), t_inv[jm]) for jm in pairs}
            t_inv = {jm: t_inv[jm] - _bdot(t_inv[jm], et[jm]) for jm in pairs}
            node *= 2
        tb = {jm: t_inv[jm].astype(BF16) for jm in pairs}
        e_gc = {jm: jnp.exp(gcb[jm]) for jm in pairs}
        ub = {jm: _bdot(tb[jm], v[jm] * beta[jm]).astype(BF16) for jm in pairs}
        wb = {(j, m): _bdot(tb[j, m], k[j, m // rep] * (beta[j, m] * e_gc[j, m])).astype(BF16)
              for j, m in pairs}
        attn = {(j, m): (qk[j, m // rep] * decay[j, m]).astype(BF16) for j, m in pairs}
        kdt = {(j, m): jnp.transpose(k[j, m // rep] * jnp.exp(gcb[j, m][c - 1:c, :] - gcb[j, m])).astype(BF16)
               for j, m in pairs}
        for j, m in pairs:
            w2_ref[m, j] = jnp.dot(kdt[j, m], wb[j, m], preferred_element_type=F32).astype(BF16)
        for j, m in pairs:
            n_ref[m, j] = jnp.dot(kdt[j, m], ub[j, m], preferred_element_type=F32)
        for j, m in pairs:
            qp_ref[m, rows[j], :] = (q[j, m // rep] * e_gc[j, m]
                                     - jnp.dot(attn[j, m], wb[j, m], preferred_element_type=F32)).astype(BF16)
        for j, m in pairs:
            o0_ref[m, rows[j], :] = jnp.dot(attn[j, m], ub[j, m], preferred_element_type=F32)

    half = max(kg // 2, 1)
    groups = [range(g0, min(g0 + half, kg)) for g0 in range(0, kg, half)]
    prepared = [prepare(khs) for khs in groups]
    for khs, prep in zip(groups, prepared):
        solve(khs, prep)

    heads = range(nh)
    nw = nw_ref[...]

    def advance(i):
        sl = pl.ds(pl.multiple_of(i * c, c), c)
        state = [state_ref[m] for m in heads]
        sb = [x.astype(BF16) for x in state]
        ws = [jnp.dot(w2_ref[m, i], sb[m], preferred_element_type=F32) for m in heads]
        o = [jnp.dot(qp_ref[m, sl, :], sb[m], preferred_element_type=F32) for m in heads]
        for m in heads:
            g_last = gc_ref[m, pl.ds(i * c + c - 1, 1), :]
            state_ref[m] = state[m] * jnp.exp(g_last) - ws[m] + n_ref[m, i]
        for m in heads:
            o0_ref[m, sl, :] += o[m]

    def finish(i):
        sl = pl.ds(pl.multiple_of(i * c, c), c)
        for m in heads:
            lanes = slice(m * hd, (m + 1) * hd)
            z = z_ref[0, sl, lanes].astype(F32)
            o_ref[0, sl, lanes] = (_rms(o0_ref[m, sl, :], nw) * _silu(z)).astype(o_ref.dtype)

    advance(0)

    def body(i, carry):
        finish(i - 1)
        advance(i)
        return carry

    lax.fori_loop(1, n_chunks, body, 0)
    finish(n_chunks - 1)


def gdn_recurrence(proj, gates, conv_w, norm_w, k_heads, v_heads):
    b, s, _ = proj.shape
    rep = v_heads // k_heads
    kg = 4 if k_heads % 4 == 0 else 1
    nh = kg * rep
    qw = kg * HEAD_DIM
    vw = nh * HEAD_DIM
    tc = _tile(s, 512)
    c = GDN_CHUNK
    n_chunks = tc // c
    steps = k_heads // kg
    v0 = 2 * k_heads * HEAD_DIM // vw
    z0 = v0 + v_heads * HEAD_DIM // vw
    conv_w = conv_w.astype(F32)

    return pl.pallas_call(
        functools.partial(_gdn_kernel, kg=kg, rep=rep, v_heads=v_heads),
        grid=(b, steps, s // tc),
        in_specs=[pl.BlockSpec((1, tc, qw), lambda i, h, t: (i, t, h)),
                  pl.BlockSpec((1, tc, qw), lambda i, h, t: (i, t, steps + h)),
                  pl.BlockSpec((1, tc, vw), lambda i, h, t: (i, t, v0 + h)),
                  pl.BlockSpec((1, tc, vw), lambda i, h, t: (i, t, z0 + h)),
                  pl.BlockSpec((1, tc, HEAD_DIM), lambda i, h, t: (i, t, 0)),
                  pl.BlockSpec((GDN_CONV, qw), lambda i, h, t: (0, h)),
                  pl.BlockSpec((GDN_CONV, qw), lambda i, h, t: (0, steps + h)),
                  pl.BlockSpec((GDN_CONV, vw), lambda i, h, t: (0, v0 + h)),
                  pl.BlockSpec((1, HEAD_DIM), lambda i, h, t: (0, 0))],
        out_specs=pl.BlockSpec((1, tc, vw), lambda i, h, t: (i, t, h)),
        out_shape=jax.ShapeDtypeStruct((b, s, v_heads * HEAD_DIM), BF16),
        scratch_shapes=[pltpu.VMEM((nh, HEAD_DIM, HEAD_DIM), F32),
                        pltpu.VMEM((tc + 8, qw), F32),
                        pltpu.VMEM((tc + 8, qw), F32),
                        pltpu.VMEM((tc + 8, vw), F32),
                        pltpu.VMEM((nh, tc, HEAD_DIM), F32),
                        pltpu.VMEM((nh, n_chunks, HEAD_DIM, HEAD_DIM), BF16),
                        pltpu.VMEM((nh, n_chunks, HEAD_DIM, HEAD_DIM), F32),
                        pltpu.VMEM((nh, tc, HEAD_DIM), BF16),
                        pltpu.VMEM((nh, tc, HEAD_DIM), F32)],
        compiler_params=_cparams(("parallel", "parallel", "arbitrary")),
        name="gdn_recurrence",
    )(proj, proj, proj, proj, gates, conv_w, conv_w, conv_w, norm_w.astype(F32).reshape(1, HEAD_DIM))


def kernel(x, c, ada_w, ada_b, norm_w, hg_w_in, hg_lb_logits, hg_norm_w, hg_w_out, gdn_w_in,
           gdn_conv_w, gdn_A_log, gdn_dt_bias, gdn_norm_w, gdn_w_out, ffn_w_gate_up, ffn_w_down):
    depth = ada_w.shape[0]
    b, s, d = x.shape
    v_heads = gdn_A_log.shape[1]
    k_heads = v_heads // 2
    gdn_main = 2 * k_heads * HEAD_DIM + 2 * v_heads * HEAD_DIM

    mod_all = ada_modulation(c, ada_w, ada_b).reshape(depth, b, N_MOD, d)
    for layer in range(depth):
        mod = mod_all[layer]
        nw = norm_w[layer]
        j = layer // 2
        if layer % 2 == 0:
            proj = norm_mod_project(x, nw[0:1], mod, hg_w_in[j].astype(BF16), 0, F32)
            mixed = hgrn2_recurrence(proj, hg_lb_logits, hg_norm_w[j], j)
            w_out = hg_w_out[j]
        else:
            w_in = gdn_w_in[j]
            w_tail = jnp.pad(w_in[:, gdn_main:], ((0, 0), (0, HEAD_DIM - 2 * v_heads)))
            proj = norm_mod_project(x, nw[0:1], mod, w_in.astype(BF16), 0, BF16, n=gdn_main)
            ba = norm_mod_project(x, nw[0:1], mod, w_tail.astype(BF16), 0, F32)
            gates = gdn_gates(ba, gdn_A_log[j], gdn_dt_bias[j], v_heads)
            mixed = gdn_recurrence(proj, gates, gdn_conv_w[j], gdn_norm_w[j], k_heads, v_heads)
            w_out = gdn_w_out[j]
        x = out_project_residual(mixed, w_out.astype(BF16), x, nw[1:2], mod, 2)
        x = ffn_residual(x, nw[2:3], nw[3:4], mod, ffn_w_gate_up[layer].astype(BF16),
                         ffn_w_down[layer].astype(BF16))
    return x
```

```python
import functools

import jax
import jax.numpy as jnp
from jax import lax
from jax.experimental import pallas as pl
from jax.experimental.pallas import tpu as pltpu

F32 = jnp.float32
BF16 = jnp.bfloat16
EPS = 1e-6
N_MOD = 6
HEAD_DIM = 128
GDN_CONV = 4
GDN_CHUNK = 64
HG_CHUNK = 128
HG_LEAF = 8
NEG = -1e30
LOG2E = 1.4426950408889634
ROWS_PER_PASS = 256
VMEM_LIMIT = 56 * 1024 * 1024


def _cparams(sem):
    return pltpu.CompilerParams(dimension_semantics=sem, vmem_limit_bytes=VMEM_LIMIT)


def _tile(n, pref):
    t = min(n, pref)
    while n % t:
        t //= 2
    return t


def _sigmoid(x):
    return 0.5 * jnp.tanh(0.5 * x) + 0.5


def _silu(x):
    h = 0.5 * x
    return h * jnp.tanh(h) + h


def _bdot(a, b):
    return jnp.dot(a.astype(BF16), b.astype(BF16), preferred_element_type=F32)


def _bdot_nt(a, b):
    return lax.dot_general(a.astype(BF16), b.astype(BF16), (((1,), (1,)), ((), ())),
                           preferred_element_type=F32)


def _bdot_tn(a, b):
    return lax.dot_general(a.astype(BF16), b.astype(BF16), (((0,), (0,)), ((), ())),
                           preferred_element_type=F32)


def _split_dot(a_exact_bf16, x):
    hi = x.astype(BF16)
    lo = (x - hi.astype(F32)).astype(BF16)
    return (jnp.dot(a_exact_bf16, hi, preferred_element_type=F32)
            + jnp.dot(a_exact_bf16, lo, preferred_element_type=F32))


def _rms(x, w):
    ms = jnp.mean(x * x, axis=-1, keepdims=True)
    return x * lax.rsqrt(ms + EPS) * w


def _ada_kernel(c_ref, w_ref, b_ref, o_ref):
    c = c_ref[...]
    o_ref[0] = _bdot(_silu(c), w_ref[0]) + b_ref[0]


def ada_modulation(c, ada_w, ada_b):
    depth, d, n = ada_w.shape
    b = c.shape[0]
    tn = _tile(n, 1024)
    return pl.pallas_call(
        _ada_kernel,
        grid=(depth, n // tn),
        in_specs=[pl.BlockSpec((b, d), lambda l, j: (0, 0)),
                  pl.BlockSpec((1, d, tn), lambda l, j: (l, 0, j)),
                  pl.BlockSpec((1, 1, tn), lambda l, j: (l, 0, j))],
        out_specs=pl.BlockSpec((1, b, tn), lambda l, j: (l, 0, j)),
        out_shape=jax.ShapeDtypeStruct((depth, b, n), F32),
        compiler_params=_cparams(("parallel", "parallel")),
        name="ada_modulation",
    )(c, ada_w, ada_b.reshape(depth, 1, n))


def _row_passes(rows, per=ROWS_PER_PASS):
    per = _tile(rows, per)
    return [slice(r * per, (r + 1) * per) for r in range(rows // per)]


def _inv_rms(load):
    x = load()
    return lax.rsqrt(jnp.mean(x * x, axis=-1, keepdims=True) + EPS)


def _norm_mod_rows(x_ref, rows, nw, mod_ref, shift_row):
    gain = nw * (1.0 + mod_ref[0, shift_row + 1:shift_row + 2, :])
    shift = mod_ref[0, shift_row:shift_row + 1, :]
    load = lambda: x_ref[0, rows, :]
    return load() * _inv_rms(load) * gain + shift


def _proj_kernel(x_ref, nw_ref, mod_ref, w_ref, o_ref, h_ref, *, shift_row):
    @pl.when(pl.program_id(2) == 0)
    def _():
        for rows in _row_passes(h_ref.shape[0]):
            h_ref[rows, :] = _norm_mod_rows(x_ref, rows, nw_ref[...], mod_ref, shift_row).astype(BF16)

    o_ref[0] = jnp.dot(h_ref[...], w_ref[...], preferred_element_type=F32).astype(o_ref.dtype)


def norm_mod_project(x, nw, mod, w, shift_row, out_dtype, n=None):
    b, s, d = x.shape
    n = w.shape[1] if n is None else n
    tm = _tile(s, 1024)
    tn = _tile(n, 1024)
    return pl.pallas_call(
        functools.partial(_proj_kernel, shift_row=shift_row),
        grid=(b, s // tm, n // tn),
        in_specs=[pl.BlockSpec((1, tm, d), lambda i, m, j: (i, m, 0)),
                  pl.BlockSpec((1, d), lambda i, m, j: (0, 0)),
                  pl.BlockSpec((1, N_MOD, d), lambda i, m, j: (i, 0, 0)),
                  pl.BlockSpec((d, tn), lambda i, m, j: (0, j))],
        out_specs=pl.BlockSpec((1, tm, tn), lambda i, m, j: (i, m, j)),
        out_shape=jax.ShapeDtypeStruct((b, s, n), out_dtype),
        scratch_shapes=[pltpu.VMEM((tm, d), BF16)],
        compiler_params=_cparams(("parallel", "parallel", "arbitrary")),
        name="norm_mod_project",
    )(x, nw, mod, w)


def _out_kernel(a_ref, w_ref, x_ref, nw_ref, mod_ref, o_ref, *, gate_row):
    o_ref[0] = jnp.dot(a_ref[0], w_ref[...], preferred_element_type=F32)
    gain = mod_ref[0, gate_row:gate_row + 1, :] * nw_ref[...]
    for rows in _row_passes(o_ref.shape[1]):
        load = lambda: o_ref[0, rows, :]
        o_ref[0, rows, :] = x_ref[0, rows, :] + load() * _inv_rms(load) * gain


def out_project_residual(a, w, x, nw, mod, gate_row):
    b, s, k = a.shape
    d = w.shape[1]
    tm = _tile(s, 512)
    return pl.pallas_call(
        functools.partial(_out_kernel, gate_row=gate_row),
        grid=(b, s // tm),
        in_specs=[pl.BlockSpec((1, tm, k), lambda i, m: (i, m, 0)),
                  pl.BlockSpec((k, d), lambda i, m: (0, 0), pipeline_mode=pl.Buffered(1)),
                  pl.BlockSpec((1, tm, d), lambda i, m: (i, m, 0)),
                  pl.BlockSpec((1, d), lambda i, m: (0, 0)),
                  pl.BlockSpec((1, N_MOD, d), lambda i, m: (i, 0, 0))],
        out_specs=pl.BlockSpec((1, tm, d), lambda i, m: (i, m, 0)),
        out_shape=jax.ShapeDtypeStruct((b, s, d), F32),
        compiler_params=_cparams(("parallel", "parallel")),
        name="out_project_residual",
    )(a, w, x, nw, mod)


def _ffn_kernel(x_ref, nw_in_ref, mod_ref, wg_ref, wu_ref, wd_ref, nw_out_ref, o_ref, h_ref):
    k = pl.program_id(2)
    tm = h_ref.shape[0]

    @pl.when(k == 0)
    def _():
        for rows in _row_passes(tm):
            h_ref[rows, :] = _norm_mod_rows(x_ref, rows, nw_in_ref[...], mod_ref, 3).astype(BF16)
        o_ref[0] = jnp.zeros(o_ref.shape[1:], F32)

    for rows in _row_passes(tm, 512):
        h = h_ref[rows, :]
        g = jnp.dot(h, wg_ref[...], preferred_element_type=F32)
        u = jnp.dot(h, wu_ref[...], preferred_element_type=F32)
        o_ref[0, rows, :] += jnp.dot((_silu(g) * u).astype(BF16), wd_ref[...], preferred_element_type=F32)

    @pl.when(k == pl.num_programs(2) - 1)
    def _():
        gain = mod_ref[0, 5:6, :] * nw_out_ref[...]
        for rows in _row_passes(tm):
            load = lambda: o_ref[0, rows, :]
            o_ref[0, rows, :] = x_ref[0, rows, :] + load() * _inv_rms(load) * gain


def ffn_residual(x, nw_in, nw_out, mod, w_gate_up, w_down):
    b, s, d = x.shape
    f = w_down.shape[0]
    tm = _tile(s, 1024)
    tf = _tile(f, 512)
    nf = f // tf
    return pl.pallas_call(
        _ffn_kernel,
        grid=(b, s // tm, nf),
        in_specs=[pl.BlockSpec((1, tm, d), lambda i, m, k: (i, m, 0)),
                  pl.BlockSpec((1, d), lambda i, m, k: (0, 0)),
                  pl.BlockSpec((1, N_MOD, d), lambda i, m, k: (i, 0, 0)),
                  pl.BlockSpec((d, tf), lambda i, m, k: (0, k)),
                  pl.BlockSpec((d, tf), lambda i, m, k: (0, k + nf)),
                  pl.BlockSpec((tf, d), lambda i, m, k: (k, 0)),
                  pl.BlockSpec((1, d), lambda i, m, k: (0, 0))],
        out_specs=pl.BlockSpec((1, tm, d), lambda i, m, k: (i, m, 0)),
        out_shape=jax.ShapeDtypeStruct((b, s, d), F32),
        scratch_shapes=[pltpu.VMEM((tm, d), BF16)],
        compiler_params=_cparams(("parallel", "parallel", "arbitrary")),
        name="ffn_residual",
    )(x, nw_in, mod, w_gate_up, w_gate_up, w_down, nw_out)


def _row_group_bcast(x, group, row):
    c, w = x.shape
    x3 = x.reshape(c // group, group, w)
    return jnp.broadcast_to(x3[:, row:row + 1, :], x3.shape).reshape(c, w)


def _hgrn_kernel(q_ref, f_ref, v_ref, g_ref, lbl_ref, nw_ref, o_ref, state_ref, kleaf_ref, bleaf_ref,
                 *, layer_j, hg):
    c = HG_CHUNK
    hd = HEAD_DIM
    n_chunks = q_ref.shape[1] // c
    pairs = [(j, m) for j in range(n_chunks) for m in range(hg)]

    @pl.when(pl.program_id(2) == 0)
    def _():
        state_ref[...] = jnp.zeros_like(state_ref)

    log_lb, log_1mlb = [], []
    for m in range(hg):
        logits = lbl_ref[m]
        e = jnp.exp(logits - jnp.max(logits, axis=0, keepdims=True))
        sm = e / jnp.sum(e, axis=0, keepdims=True)
        lb = jnp.sum(sm[:layer_j + 1], axis=0, keepdims=True) - sm[0:1]
        log_lb.append(jnp.log(lb))
        log_1mlb.append(jnp.log1p(-lb))

    row = lax.broadcasted_iota(jnp.int32, (c, c), 0)
    col = lax.broadcasted_iota(jnp.int32, (c, c), 1)
    rowk = lax.broadcasted_iota(jnp.int32, (c, hd), 0)
    tril = (row >= col).astype(BF16)

    def blk(ref, j, m):
        return ref[0, j * c:(j + 1) * c, m * hd:(m + 1) * hd]

    q = {(j, m): _silu(blk(q_ref, j, m)) for j, m in pairs}
    v = {(j, m): blk(v_ref, j, m) for j, m in pairs}
    log_f = {}
    for j, m in pairs:
        f_raw = blk(f_ref, j, m)
        log_sig = jnp.minimum(f_raw, 0.0) - jnp.log(1.0 + jnp.exp(-jnp.abs(f_raw)))
        cand = log_1mlb[m] + log_sig
        log_f[j, m] = jnp.maximum(log_lb[m], cand) + jnp.log(1.0 + jnp.exp(-jnp.abs(log_lb[m] - cand)))
    k = {jm: 1.0 - jnp.exp(log_f[jm]) for jm in pairs}
    b = {jm: _split_dot(tril, log_f[jm]) * LOG2E for jm in pairs}
    b_last = {jm: b[jm][c - 1:c, :] for jm in pairs}

    scores = {jm: jnp.zeros((c, c), F32) for jm in pairs}
    n = c
    while n > HG_LEAF:
        half = n // 2
        right = (rowk % n) >= half
        same = (row // n) == (col // n)
        qt, kt = {}, {}
        for jm in pairs:
            r = _row_group_bcast(b[jm], n, half - 1)
            qt[jm] = q[jm] * jnp.exp2(jnp.where(right, b[jm] - r, NEG))
            kt[jm] = k[jm] * jnp.exp2(jnp.where(right, NEG, r - b[jm]))
        scores = {jm: scores[jm] + jnp.where(same, _bdot_nt(qt[jm], kt[jm]), 0.0) for jm in pairs}
        n = half

    g = c // HG_LEAF
    t_in = lax.broadcasted_iota(jnp.int32, (g, HG_LEAF, hd), 1)
    col3 = lax.broadcasted_iota(jnp.int32, (g, HG_LEAF, c), 2)
    blk0 = lax.broadcasted_iota(jnp.int32, (g, HG_LEAF, c), 0) * HG_LEAF
    for p_idx, jm in enumerate(pairs):
        q3 = q[jm].reshape(g, HG_LEAF, hd)
        b3 = b[jm].reshape(g, HG_LEAF, hd)
        s3 = scores[jm].reshape(g, HG_LEAF, c)
        kleaf_ref[p_idx] = k[jm].reshape(g, HG_LEAF, hd)
        bleaf_ref[p_idx] = b3
        for s in range(HG_LEAF):
            ks = kleaf_ref[p_idx, :, pl.ds(s, 1), :]
            bs = bleaf_ref[p_idx, :, pl.ds(s, 1), :]
            p = q3 * ks * jnp.exp2(jnp.where(t_in >= s, b3 - bs, NEG))
            s3 = jnp.where(col3 == blk0 + s, jnp.sum(p, axis=-1, keepdims=True), s3)
        scores[jm] = s3.reshape(c, c)
    o = {jm: _bdot(scores[jm], v[jm]) for jm in pairs}

    qe = {jm: (q[jm] * jnp.exp2(b[jm])).astype(BF16) for jm in pairs}
    upd = {jm: _bdot_tn(v[jm], k[jm] * jnp.exp2(b_last[jm] - b[jm])) for jm in pairs}

    state = [state_ref[m] for m in range(hg)]
    for j in range(n_chunks):
        for m in range(hg):
            o[j, m] = o[j, m] + _bdot_nt(qe[j, m], state[m])
            state[m] = state[m] * jnp.exp2(b_last[j, m]) + upd[j, m]
    for m in range(hg):
        state_ref[m] = state[m]

    nw = nw_ref[...]
    for j, m in pairs:
        gate = _sigmoid(blk(g_ref, j, m))
        o_ref[0, j * c:(j + 1) * c, m * hd:(m + 1) * hd] = (_rms(o[j, m], nw) * gate).astype(o_ref.dtype)


def hgrn2_recurrence(proj, lb_logits, norm_w, layer_j):
    b, s, n4 = proj.shape
    hd = n4 // 4
    heads = hd // HEAD_DIM
    hg = 2 if heads % 2 == 0 else 1
    steps = heads // hg
    tc = _tile(s, 512)
    nl = lb_logits.shape[0]
    lbl = lb_logits.astype(F32).reshape(nl, heads, HEAD_DIM).transpose(1, 0, 2)
    leaf_shape = (hg * (tc // HG_CHUNK), HG_CHUNK // HG_LEAF, HG_LEAF, HEAD_DIM)

    def col(section):
        return pl.BlockSpec((1, tc, hg * HEAD_DIM), lambda i, h, t: (i, t, section * steps + h))

    return pl.pallas_call(
        functools.partial(_hgrn_kernel, layer_j=layer_j, hg=hg),
        grid=(b, steps, s // tc),
        in_specs=[col(0), col(1), col(2), col(3),
                  pl.BlockSpec((hg, nl, HEAD_DIM), lambda i, h, t: (h, 0, 0)),
                  pl.BlockSpec((1, HEAD_DIM), lambda i, h, t: (0, 0))],
        out_specs=pl.BlockSpec((1, tc, hg * HEAD_DIM), lambda i, h, t: (i, t, h)),
        out_shape=jax.ShapeDtypeStruct((b, s, hd), BF16),
        scratch_shapes=[pltpu.VMEM((hg, HEAD_DIM, HEAD_DIM), F32),
                        pltpu.VMEM(leaf_shape, F32),
                        pltpu.VMEM(leaf_shape, F32)],
        compiler_params=_cparams(("parallel", "parallel", "arbitrary")),
        name="hgrn2_recurrence",
    )(proj, proj, proj, proj, lbl, norm_w.astype(F32).reshape(1, HEAD_DIM))


def _l2n(x):
    return x * lax.rsqrt(jnp.sum(x * x, axis=-1, keepdims=True) + EPS)


def _gdn_gates_kernel(ba_ref, ga_ref, o_ref, *, v_heads):
    ba = ba_ref[0]
    t = ba.shape[0]
    beta = _sigmoid(ba)
    xa = ba + ga_ref[1:2, :]
    g = -jnp.exp(ga_ref[0:1, :]) * (jnp.maximum(xa, 0.0) + jnp.log1p(jnp.exp(-jnp.abs(xa))))
    row = lax.broadcasted_iota(jnp.int32, (t, t), 0)
    col = lax.broadcasted_iota(jnp.int32, (t, t), 1)
    block_tril = jnp.where((row >= col) & (row // GDN_CHUNK == col // GDN_CHUNK), 1.0, 0.0).astype(BF16)
    gc = _split_dot(block_tril, g)
    lane = lax.broadcasted_iota(jnp.int32, ba.shape, 1)
    o_ref[0] = jnp.where(lane < v_heads, beta, gc)


def gdn_gates(ba, a_log, dt_bias, v_heads):
    b, s, w = ba.shape
    tc = _tile(s, 512)
    pad = w - 2 * v_heads
    zeros = jnp.zeros((v_heads,), F32)
    gate_params = jnp.stack([
        jnp.concatenate([zeros, a_log.astype(F32), jnp.zeros((pad,), F32)]),
        jnp.concatenate([zeros, dt_bias.astype(F32), jnp.zeros((pad,), F32)]),
    ])
    return pl.pallas_call(
        functools.partial(_gdn_gates_kernel, v_heads=v_heads),
        grid=(b, s // tc),
        in_specs=[pl.BlockSpec((1, tc, w), lambda i, t: (i, t, 0)),
                  pl.BlockSpec((2, w), lambda i, t: (0, 0))],
        out_specs=pl.BlockSpec((1, tc, w), lambda i, t: (i, t, 0)),
        out_shape=jax.ShapeDtypeStruct((b, s, w), F32),
        compiler_params=_cparams(("parallel", "parallel")),
        name="gdn_gates",
    )(ba, gate_params)


def _conv_silu(xp_ref, w, lanes, nrows):
    out = xp_ref[8:8 + nrows, lanes] * w[GDN_CONV - 1:GDN_CONV, :]
    for back in range(1, GDN_CONV):
        out = out + xp_ref[pl.ds(8 - back, nrows), lanes] * w[GDN_CONV - 1 - back:GDN_CONV - back, :]
    return _silu(out)


def _gdn_kernel(q_ref, k_ref, v_ref, z_ref, gates_ref, cq_ref, ck_ref, cv_ref, nw_ref,
                o_ref, state_ref, xq_ref, xk_ref, xv_ref, gc_ref,
                w2_ref, n_ref, qp_ref, o0_ref, *, kg, rep, v_heads):
    tc = q_ref.shape[1]
    hd = HEAD_DIM
    c = GDN_CHUNK
    n_chunks = tc // c
    nh = kg * rep

    @pl.when(pl.program_id(2) == 0)
    def _():
        state_ref[...] = jnp.zeros_like(state_ref)
        xq_ref[tc:, :] = jnp.zeros((8, xq_ref.shape[1]), F32)
        xk_ref[tc:, :] = jnp.zeros((8, xk_ref.shape[1]), F32)
        xv_ref[tc:, :] = jnp.zeros((8, xv_ref.shape[1]), F32)

    row = lax.broadcasted_iota(jnp.int32, (c, c), 0)
    col = lax.broadcasted_iota(jnp.int32, (c, c), 1)
    eye = (row == col).astype(F32)
    rows = [slice(j * c, (j + 1) * c) for j in range(n_chunks)]

    for xp_ref, raw_ref in ((xq_ref, q_ref), (xk_ref, k_ref), (xv_ref, v_ref)):
        xp_ref[0:8, :] = xp_ref[tc:tc + 8, :]
        xp_ref[8:, :] = raw_ref[0].astype(F32)

    def prepare(khs):
        q, k, v, beta, gcb = {}, {}, {}, {}, {}
        gates = gates_ref[0]
        lane = lax.broadcasted_iota(jnp.int32, gates.shape, 1)
        for g in khs:
            lanes = slice(g * hd, (g + 1) * hd)
            qn = _l2n(_conv_silu(xq_ref, cq_ref[:, lanes], lanes, tc)) * (hd ** -0.5)
            kn = _l2n(_conv_silu(xk_ref, ck_ref[:, lanes], lanes, tc))
            for j in range(n_chunks):
                q[j, g] = qn[rows[j], :]
                k[j, g] = kn[rows[j], :]
            for m in range(g * rep, (g + 1) * rep):
                vlanes = slice(m * hd, (m + 1) * hd)
                va = _conv_silu(xv_ref, cv_ref[:, vlanes], vlanes, tc)
                vh = pl.program_id(1) * nh + m
                beta_col = jnp.sum(jnp.where(lane == vh, gates, 0.0), axis=-1, keepdims=True)
                gc_col = jnp.sum(jnp.where(lane == vh + v_heads, gates, 0.0), axis=-1, keepdims=True)
                beta_b = jnp.broadcast_to(beta_col, (tc, hd))
                gc_b = jnp.broadcast_to(gc_col, (tc, hd))
                gc_ref[m] = gc_b
                for j in range(n_chunks):
                    v[j, m] = va[rows[j], :]
                    beta[j, m] = beta_b[rows[j], :]
                    gcb[j, m] = gc_b[rows[j], :]
        return q, k, v, beta, gcb

    def solve(khs, prepared):
        q, k, v, beta, gcb = prepared
        pairs = [(j, m) for j in range(n_chunks) for g in khs for m in range(g * rep, (g + 1) * rep)]
        kpairs = [(j, g) for j in range(n_chunks) for g in khs]
        kb = {jg: k[jg].astype(BF16) for jg in kpairs}
        kk = {jg: _bdot_nt(kb[jg], kb[jg]) for jg in kpairs}
        qk = {jg: _bdot_nt(q[jg], kb[jg]) for jg in kpairs}
        decay = {jm: jnp.exp(jnp.where(row >= col, gcb[jm][:, 0:c] - jnp.transpose(gcb[jm])[0:1, :], NEG))
                 for jm in pairs}
        a = {(j, m): beta[j, m][:, 0:c] * kk[j, m // rep] * decay[j, m] for j, m in pairs}
        t_inv = {jm: eye - jnp.where((row // 2 == col // 2) & (row > col), a[jm], 0.0) for jm in pairs}
        ab = {jm: a[jm].astype(BF16) for jm in pairs}
        tb = {jm: t_inv[jm].astype(BF16) for jm in pairs}
        zero = jnp.zeros((c, c), BF16)
        node = 4
        while node <= c:
            links = (row // node == col // node) & (row % node >= node // 2) & (col % node < node // 2)
            et = {jm: jnp.dot(jnp.where(links, ab[jm], zero), tb[jm], preferred_element_type=F32).astype(BF16)
                  for jm in pairs}
            t_inv = {jm: t_inv[jm] - jnp.dot(tb[jm], et[jm], preferred_element_type=F32) for jm in pairs}
            tb = {jm: t_inv[jm].astype(BF16) for jm in pairs}
            node *= 2
        e_gc = {jm: jnp.exp(gcb[jm]) for jm in pairs}
        ub = {jm: _bdot(tb[jm], v[jm] * beta[jm]).astype(BF16) for jm in pairs}
        wb = {(j, m): _bdot(tb[j, m], k[j, m // rep] * (beta[j, m] * e_gc[j, m])).astype(BF16)
              for j, m in pairs}
        attn = {(j, m): (qk[j, m // rep] * decay[j, m]).astype(BF16) for j, m in pairs}
        kdt = {(j, m): jnp.transpose(k[j, m // rep] * jnp.exp(gcb[j, m][c - 1:c, :] - gcb[j, m])).astype(BF16)
               for j, m in pairs}
        for j, m in pairs:
            w2_ref[m, j] = jnp.dot(kdt[j, m], wb[j, m], preferred_element_type=F32).astype(BF16)
        for j, m in pairs:
            n_ref[m, j] = jnp.dot(kdt[j, m], ub[j, m], preferred_element_type=F32)
        for j, m in pairs:
            qp_ref[m, rows[j], :] = (q[j, m // rep] * e_gc[j, m]
                                     - jnp.dot(attn[j, m], wb[j, m], preferred_element_type=F32)).astype(BF16)
        for j, m in pairs:
            o0_ref[m, rows[j], :] = jnp.dot(attn[j, m], ub[j, m], preferred_element_type=F32)

    half = max(kg // 2, 1)
    groups = [range(g0, min(g0 + half, kg)) for g0 in range(0, kg, half)]
    prepared = [prepare(khs) for khs in groups]
    for khs, prep in zip(groups, prepared):
        solve(khs, prep)

    heads = range(nh)
    nw = nw_ref[...]

    def advance(i):
        sl = pl.ds(pl.multiple_of(i * c, c), c)
        state = [state_ref[m] for m in heads]
        sb = [x.astype(BF16) for x in state]
        ws = [jnp.dot(w2_ref[m, i], sb[m], preferred_element_type=F32) for m in heads]
        o = [jnp.dot(qp_ref[m, sl, :], sb[m], preferred_element_type=F32) for m in heads]
        for m in heads:
            g_last = gc_ref[m, pl.ds(i * c + c - 1, 1), :]
            state_ref[m] = state[m] * jnp.exp(g_last) - ws[m] + n_ref[m, i]
        for m in heads:
            o0_ref[m, sl, :] += o[m]

    def finish(i):
        sl = pl.ds(pl.multiple_of(i * c, c), c)
        for m in heads:
            lanes = slice(m * hd, (m + 1) * hd)
            z = z_ref[0, sl, lanes].astype(F32)
            o_ref[0, sl, lanes] = (_rms(o0_ref[m, sl, :], nw) * _silu(z)).astype(o_ref.dtype)

    advance(0)

    def body(i, carry):
        finish(i - 1)
        advance(i)
        return carry

    lax.fori_loop(1, n_chunks, body, 0)
    finish(n_chunks - 1)


def gdn_recurrence(proj, gates, conv_w, norm_w, k_heads, v_heads):
    b, s, _ = proj.shape
    rep = v_heads // k_heads
    kg = 4 if k_heads % 4 == 0 else 1
    nh = kg * rep
    qw = kg * HEAD_DIM
    vw = nh * HEAD_DIM
    tc = _tile(s, 512)
    c = GDN_CHUNK
    n_chunks = tc // c
    steps = k_heads // kg
    v0 = 2 * k_heads * HEAD_DIM // vw
    z0 = v0 + v_heads * HEAD_DIM // vw
    conv_w = conv_w.astype(F32)

    return pl.pallas_call(
        functools.partial(_gdn_kernel, kg=kg, rep=rep, v_heads=v_heads),
        grid=(b, steps, s // tc),
        in_specs=[pl.BlockSpec((1, tc, qw), lambda i, h, t: (i, t, h)),
                  pl.BlockSpec((1, tc, qw), lambda i, h, t: (i, t, steps + h)),
                  pl.BlockSpec((1, tc, vw), lambda i, h, t: (i, t, v0 + h)),
                  pl.BlockSpec((1, tc, vw), lambda i, h, t: (i, t, z0 + h)),
                  pl.BlockSpec((1, tc, HEAD_DIM), lambda i, h, t: (i, t, 0)),
                  pl.BlockSpec((GDN_CONV, qw), lambda i, h, t: (0, h)),
                  pl.BlockSpec((GDN_CONV, qw), lambda i, h, t: (0, steps + h)),
                  pl.BlockSpec((GDN_CONV, vw), lambda i, h, t: (0, v0 + h)),
                  pl.BlockSpec((1, HEAD_DIM), lambda i, h, t: (0, 0))],
        out_specs=pl.BlockSpec((1, tc, vw), lambda i, h, t: (i, t, h)),
        out_shape=jax.ShapeDtypeStruct((b, s, v_heads * HEAD_DIM), BF16),
        scratch_shapes=[pltpu.VMEM((nh, HEAD_DIM, HEAD_DIM), F32),
                        pltpu.VMEM((tc + 8, qw), F32),
                        pltpu.VMEM((tc + 8, qw), F32),
                        pltpu.VMEM((tc + 8, vw), F32),
                        pltpu.VMEM((nh, tc, HEAD_DIM), F32),
                        pltpu.VMEM((nh, n_chunks, HEAD_DIM, HEAD_DIM), BF16),
                        pltpu.VMEM((nh, n_chunks, HEAD_DIM, HEAD_DIM), F32),
                        pltpu.VMEM((nh, tc, HEAD_DIM), BF16),
                        pltpu.VMEM((nh, tc, HEAD_DIM), F32)],
        compiler_params=_cparams(("parallel", "parallel", "arbitrary")),
        name="gdn_recurrence",
    )(proj, proj, proj, proj, gates, conv_w, conv_w, conv_w, norm_w.astype(F32).reshape(1, HEAD_DIM))


def kernel(x, c, ada_w, ada_b, norm_w, hg_w_in, hg_lb_logits, hg_norm_w, hg_w_out, gdn_w_in,
           gdn_conv_w, gdn_A_log, gdn_dt_bias, gdn_norm_w, gdn_w_out, ffn_w_gate_up, ffn_w_down):
    depth = ada_w.shape[0]
    b, s, d = x.shape
    v_heads = gdn_A_log.shape[1]
    k_heads = v_heads // 2
    gdn_main = 2 * k_heads * HEAD_DIM + 2 * v_heads * HEAD_DIM

    mod_all = ada_modulation(c, ada_w, ada_b).reshape(depth, b, N_MOD, d)
    for layer in range(depth):
        mod = mod_all[layer]
        nw = norm_w[layer]
        j = layer // 2
        if layer % 2 == 0:
            proj = norm_mod_project(x, nw[0:1], mod, hg_w_in[j].astype(BF16), 0, F32)
            mixed = hgrn2_recurrence(proj, hg_lb_logits, hg_norm_w[j], j)
            w_out = hg_w_out[j]
        else:
            w_in = gdn_w_in[j]
            w_tail = jnp.pad(w_in[:, gdn_main:], ((0, 0), (0, HEAD_DIM - 2 * v_heads)))
            proj = norm_mod_project(x, nw[0:1], mod, w_in.astype(BF16), 0, BF16, n=gdn_main)
            ba = norm_mod_project(x, nw[0:1], mod, w_tail.astype(BF16), 0, F32)
            gates = gdn_gates(ba, gdn_A_log[j], gdn_dt_bias[j], v_heads)
            mixed = gdn_recurrence(proj, gates, gdn_conv_w[j], gdn_norm_w[j], k_heads, v_heads)
            w_out = gdn_w_out[j]
        x = out_project_residual(mixed, w_out.astype(BF16), x, nw[1:2], mod, 2)
        x = ffn_residual(x, nw[2:3], nw[3:4], mod, ffn_w_gate_up[layer].astype(BF16),
                         ffn_w_down[layer].astype(BF16))
    return x
```

```python
import functools

import jax
import jax.numpy as jnp
from jax import lax
from jax.experimental import pallas as pl
from jax.experimental.pallas import tpu as pltpu

F32 = jnp.float32
BF16 = jnp.bfloat16
EPS = 1e-6
N_MOD = 6
HEAD_DIM = 128
GDN_CONV = 4
GDN_CHUNK = 64
HG_CHUNK = 128
HG_LEAF = 8
NEG = -1e30
LOG2E = 1.4426950408889634
ROWS_PER_PASS = 256
VMEM_LIMIT = 56 * 1024 * 1024


def _cparams(sem):
    return pltpu.CompilerParams(dimension_semantics=sem, vmem_limit_bytes=VMEM_LIMIT)


def _tile(n, pref):
    t = min(n, pref)
    while n % t:
        t //= 2
    return t


def _sigmoid(x):
    return 0.5 * jnp.tanh(0.5 * x) + 0.5


def _silu(x):
    h = 0.5 * x
    return h * jnp.tanh(h) + h


def _bdot(a, b):
    return jnp.dot(a.astype(BF16), b.astype(BF16), preferred_element_type=F32)


def _bdot_nt(a, b):
    return lax.dot_general(a.astype(BF16), b.astype(BF16), (((1,), (1,)), ((), ())),
                           preferred_element_type=F32)


def _bdot_tn(a, b):
    return lax.dot_general(a.astype(BF16), b.astype(BF16), (((0,), (0,)), ((), ())),
                           preferred_element_type=F32)


def _split_dot(a_exact_bf16, x):
    hi = x.astype(BF16)
    lo = (x - hi.astype(F32)).astype(BF16)
    return (jnp.dot(a_exact_bf16, hi, preferred_element_type=F32)
            + jnp.dot(a_exact_bf16, lo, preferred_element_type=F32))


def _rms(x, w):
    ms = jnp.mean(x * x, axis=-1, keepdims=True)
    return x * lax.rsqrt(ms + EPS) * w


def _ada_kernel(c_ref, w_ref, b_ref, o_ref):
    c = c_ref[...]
    o_ref[0] = _bdot(_silu(c), w_ref[0]) + b_ref[0]


def ada_modulation(c, ada_w, ada_b):
    depth, d, n = ada_w.shape
    b = c.shape[0]
    tn = _tile(n, 1024)
    return pl.pallas_call(
        _ada_kernel,
        grid=(depth, n // tn),
        in_specs=[pl.BlockSpec((b, d), lambda l, j: (0, 0)),
                  pl.BlockSpec((1, d, tn), lambda l, j: (l, 0, j)),
                  pl.BlockSpec((1, 1, tn), lambda l, j: (l, 0, j))],
        out_specs=pl.BlockSpec((1, b, tn), lambda l, j: (l, 0, j)),
        out_shape=jax.ShapeDtypeStruct((depth, b, n), F32),
        compiler_params=_cparams(("parallel", "parallel")),
        name="ada_modulation",
    )(c, ada_w, ada_b.reshape(depth, 1, n))


def _row_passes(rows, per=ROWS_PER_PASS):
    per = _tile(rows, per)
    return [slice(r * per, (r + 1) * per) for r in range(rows // per)]


def _inv_rms(load):
    x = load()
    return lax.rsqrt(jnp.mean(x * x, axis=-1, keepdims=True) + EPS)


def _norm_mod_rows(x_ref, rows, nw, mod_ref, shift_row):
    gain = nw * (1.0 + mod_ref[0, shift_row + 1:shift_row + 2, :])
    shift = mod_ref[0, shift_row:shift_row + 1, :]
    load = lambda: x_ref[0, rows, :]
    return load() * _inv_rms(load) * gain + shift


def _proj_kernel(x_ref, nw_ref, mod_ref, w_ref, *rest, shift_row, with_tail):
    if with_tail:
        wt_ref, o_ref, ot_ref, h_ref = rest
    else:
        o_ref, h_ref = rest
    tm = h_ref.shape[0]

    def body(first):
        for rows in _row_passes(tm, 512):
            if first:
                for sub in _row_passes(rows.stop - rows.start):
                    rr = slice(rows.start + sub.start, rows.start + sub.stop)
                    h_ref[rr, :] = _norm_mod_rows(x_ref, rr, nw_ref[...], mod_ref, shift_row).astype(BF16)
            o_ref[0, rows, :] = jnp.dot(h_ref[rows, :], w_ref[...],
                                        preferred_element_type=F32).astype(o_ref.dtype)
            if first and with_tail:
                ot_ref[0, rows, :] = jnp.dot(h_ref[rows, :], wt_ref[...], preferred_element_type=F32)

    @pl.when(pl.program_id(2) == 0)
    def _():
        body(True)

    @pl.when(pl.program_id(2) > 0)
    def _():
        body(False)


def norm_mod_project(x, nw, mod, w, shift_row, out_dtype, n=None, w_tail=None):
    b, s, d = x.shape
    n = w.shape[1] if n is None else n
    tm = _tile(s, 1024)
    tn = _tile(n, 1024)
    in_specs = [pl.BlockSpec((1, tm, d), lambda i, m, j: (i, m, 0)),
                pl.BlockSpec((1, d), lambda i, m, j: (0, 0)),
                pl.BlockSpec((1, N_MOD, d), lambda i, m, j: (i, 0, 0)),
                pl.BlockSpec((d, tn), lambda i, m, j: (0, j))]
    out_specs = pl.BlockSpec((1, tm, tn), lambda i, m, j: (i, m, j))
    out_shape = jax.ShapeDtypeStruct((b, s, n), out_dtype)
    args = (x, nw, mod, w)
    if w_tail is not None:
        nt = w_tail.shape[1]
        in_specs.append(pl.BlockSpec((d, nt), lambda i, m, j: (0, 0)))
        out_specs = [out_specs, pl.BlockSpec((1, tm, nt), lambda i, m, j: (i, m, 0))]
        out_shape = [out_shape, jax.ShapeDtypeStruct((b, s, nt), F32)]
        args = args + (w_tail,)
    return pl.pallas_call(
        functools.partial(_proj_kernel, shift_row=shift_row, with_tail=w_tail is not None),
        grid=(b, s // tm, n // tn),
        in_specs=in_specs,
        out_specs=out_specs,
        out_shape=out_shape,
        scratch_shapes=[pltpu.VMEM((tm, d), BF16)],
        compiler_params=_cparams(("parallel", "parallel", "arbitrary")),
        name="norm_mod_project",
    )(*args)


def _out_kernel(a_ref, w_ref, x_ref, nw_ref, mod_ref, o_ref, *, gate_row):
    o_ref[0] = jnp.dot(a_ref[0], w_ref[...], preferred_element_type=F32)
    gain = mod_ref[0, gate_row:gate_row + 1, :] * nw_ref[...]
    for rows in _row_passes(o_ref.shape[1]):
        load = lambda: o_ref[0, rows, :]
        o_ref[0, rows, :] = x_ref[0, rows, :] + load() * _inv_rms(load) * gain


def out_project_residual(a, w, x, nw, mod, gate_row):
    b, s, k = a.shape
    d = w.shape[1]
    tm = _tile(s, 512)
    return pl.pallas_call(
        functools.partial(_out_kernel, gate_row=gate_row),
        grid=(b, s // tm),
        in_specs=[pl.BlockSpec((1, tm, k), lambda i, m: (i, m, 0)),
                  pl.BlockSpec((k, d), lambda i, m: (0, 0), pipeline_mode=pl.Buffered(1)),
                  pl.BlockSpec((1, tm, d), lambda i, m: (i, m, 0)),
                  pl.BlockSpec((1, d), lambda i, m: (0, 0)),
                  pl.BlockSpec((1, N_MOD, d), lambda i, m: (i, 0, 0))],
        out_specs=pl.BlockSpec((1, tm, d), lambda i, m: (i, m, 0)),
        out_shape=jax.ShapeDtypeStruct((b, s, d), F32),
        compiler_params=_cparams(("parallel", "parallel")),
        name="out_project_residual",
    )(a, w, x, nw, mod)


def _ffn_kernel(x_ref, nw_in_ref, mod_ref, wg_ref, wu_ref, wd_ref, nw_out_ref, o_ref, h_ref):
    k = pl.program_id(2)
    nk = pl.num_programs(2)
    tm = h_ref.shape[0]

    def body(first, last):
        for rows in _row_passes(tm, 512):
            if first:
                for sub in _row_passes(rows.stop - rows.start):
                    rr = slice(rows.start + sub.start, rows.start + sub.stop)
                    h_ref[rr, :] = _norm_mod_rows(x_ref, rr, nw_in_ref[...], mod_ref, 3).astype(BF16)
            h = h_ref[rows, :]
            g = jnp.dot(h, wg_ref[...], preferred_element_type=F32)
            u = jnp.dot(h, wu_ref[...], preferred_element_type=F32)
            part = jnp.dot((_silu(g) * u).astype(BF16), wd_ref[...], preferred_element_type=F32)
            if first:
                o_ref[0, rows, :] = part
            else:
                o_ref[0, rows, :] += part
            if last:
                gain = mod_ref[0, 5:6, :] * nw_out_ref[...]
                for sub in _row_passes(rows.stop - rows.start):
                    rr = slice(rows.start + sub.start, rows.start + sub.stop)
                    load = lambda: o_ref[0, rr, :]
                    o_ref[0, rr, :] = x_ref[0, rr, :] + load() * _inv_rms(load) * gain

    @pl.when((k == 0) & (nk > 1))
    def _():
        body(True, False)

    @pl.when((k > 0) & (k < nk - 1))
    def _():
        body(False, False)

    @pl.when((k == nk - 1) & (nk > 1))
    def _():
        body(False, True)

    @pl.when(nk == 1)
    def _():
        body(True, True)


def ffn_residual(x, nw_in, nw_out, mod, w_gate_up, w_down):
    b, s, d = x.shape
    f = w_down.shape[0]
    tm = _tile(s, 1024)
    tf = _tile(f, 512)
    nf = f // tf
    return pl.pallas_call(
        _ffn_kernel,
        grid=(b, s // tm, nf),
        in_specs=[pl.BlockSpec((1, tm, d), lambda i, m, k: (i, m, 0)),
                  pl.BlockSpec((1, d), lambda i, m, k: (0, 0)),
                  pl.BlockSpec((1, N_MOD, d), lambda i, m, k: (i, 0, 0)),
                  pl.BlockSpec((d, tf), lambda i, m, k: (0, k)),
                  pl.BlockSpec((d, tf), lambda i, m, k: (0, k + nf)),
                  pl.BlockSpec((tf, d), lambda i, m, k: (k, 0)),
                  pl.BlockSpec((1, d), lambda i, m, k: (0, 0))],
        out_specs=pl.BlockSpec((1, tm, d), lambda i, m, k: (i, m, 0)),
        out_shape=jax.ShapeDtypeStruct((b, s, d), F32),
        scratch_shapes=[pltpu.VMEM((tm, d), BF16)],
        compiler_params=_cparams(("parallel", "parallel", "arbitrary")),
        name="ffn_residual",
    )(x, nw_in, mod, w_gate_up, w_gate_up, w_down, nw_out)


def _row_group_bcast(x, group, row):
    c, w = x.shape
    x3 = x.reshape(c // group, group, w)
    return jnp.broadcast_to(x3[:, row:row + 1, :], x3.shape).reshape(c, w)


def _hgrn_kernel(q_ref, f_ref, v_ref, g_ref, lbl_ref, nw_ref, o_ref, state_ref, kleaf_ref, bleaf_ref,
                 *, layer_j, hg):
    c = HG_CHUNK
    hd = HEAD_DIM
    n_chunks = q_ref.shape[1] // c
    pairs = [(j, m) for j in range(n_chunks) for m in range(hg)]

    @pl.when(pl.program_id(2) == 0)
    def _():
        state_ref[...] = jnp.zeros_like(state_ref)

    log_lb, log_1mlb = [], []
    for m in range(hg):
        logits = lbl_ref[m]
        e = jnp.exp(logits - jnp.max(logits, axis=0, keepdims=True))
        sm = e / jnp.sum(e, axis=0, keepdims=True)
        lb = jnp.sum(sm[:layer_j + 1], axis=0, keepdims=True) - sm[0:1]
        log_lb.append(jnp.log(lb))
        log_1mlb.append(jnp.log1p(-lb))

    row = lax.broadcasted_iota(jnp.int32, (c, c), 0)
    col = lax.broadcasted_iota(jnp.int32, (c, c), 1)
    rowk = lax.broadcasted_iota(jnp.int32, (c, hd), 0)
    tril = (row >= col).astype(BF16)

    def blk(ref, j, m):
        return ref[0, j * c:(j + 1) * c, m * hd:(m + 1) * hd]

    q = {(j, m): _silu(blk(q_ref, j, m)) for j, m in pairs}
    v = {(j, m): blk(v_ref, j, m) for j, m in pairs}
    log_f = {}
    for j, m in pairs:
        f_raw = blk(f_ref, j, m)
        log_sig = jnp.minimum(f_raw, 0.0) - jnp.log(1.0 + jnp.exp(-jnp.abs(f_raw)))
        cand = log_1mlb[m] + log_sig
        log_f[j, m] = jnp.maximum(log_lb[m], cand) + jnp.log(1.0 + jnp.exp(-jnp.abs(log_lb[m] - cand)))
    k = {jm: 1.0 - jnp.exp(log_f[jm]) for jm in pairs}
    b = {jm: _split_dot(tril, log_f[jm]) * LOG2E for jm in pairs}
    b_last = {jm: b[jm][c - 1:c, :] for jm in pairs}

    scores = {jm: jnp.zeros((c, c), F32) for jm in pairs}
    n = c
    while n > HG_LEAF:
        half = n // 2
        right = (rowk % n) >= half
        same = (row // n) == (col // n)
        qt, kt = {}, {}
        for jm in pairs:
            r = _row_group_bcast(b[jm], n, half - 1)
            qt[jm] = q[jm] * jnp.exp2(jnp.where(right, b[jm] - r, NEG))
            kt[jm] = k[jm] * jnp.exp2(jnp.where(right, NEG, r - b[jm]))
        scores = {jm: scores[jm] + jnp.where(same, _bdot_nt(qt[jm], kt[jm]), 0.0) for jm in pairs}
        n = half

    g = c // HG_LEAF
    t_in = lax.broadcasted_iota(jnp.int32, (g, HG_LEAF, hd), 1)
    col3 = lax.broadcasted_iota(jnp.int32, (g, HG_LEAF, c), 2)
    blk0 = lax.broadcasted_iota(jnp.int32, (g, HG_LEAF, c), 0) * HG_LEAF
    for p_idx, jm in enumerate(pairs):
        q3 = q[jm].reshape(g, HG_LEAF, hd)
        b3 = b[jm].reshape(g, HG_LEAF, hd)
        s3 = scores[jm].reshape(g, HG_LEAF, c)
        kleaf_ref[p_idx] = k[jm].reshape(g, HG_LEAF, hd)
        bleaf_ref[p_idx] = b3
        for s in range(HG_LEAF):
            ks = kleaf_ref[p_idx, :, pl.ds(s, 1), :]
            bs = bleaf_ref[p_idx, :, pl.ds(s, 1), :]
            p = q3 * ks * jnp.exp2(jnp.where(t_in >= s, b3 - bs, NEG))
            s3 = jnp.where(col3 == blk0 + s, jnp.sum(p, axis=-1, keepdims=True), s3)
        scores[jm] = s3.reshape(c, c)
    o = {jm: _bdot(scores[jm], v[jm]) for jm in pairs}

    qe = {jm: (q[jm] * jnp.exp2(b[jm])).astype(BF16) for jm in pairs}
    upd = {jm: _bdot_tn(v[jm], k[jm] * jnp.exp2(b_last[jm] - b[jm])) for jm in pairs}

    state = [state_ref[m] for m in range(hg)]
    for j in range(n_chunks):
        for m in range(hg):
            o[j, m] = o[j, m] + _bdot_nt(qe[j, m], state[m])
            state[m] = state[m] * jnp.exp2(b_last[j, m]) + upd[j, m]
    for m in range(hg):
        state_ref[m] = state[m]

    nw = nw_ref[...]
    for j, m in pairs:
        gate = _sigmoid(blk(g_ref, j, m))
        o_ref[0, j * c:(j + 1) * c, m * hd:(m + 1) * hd] = (_rms(o[j, m], nw) * gate).astype(o_ref.dtype)


def hgrn2_recurrence(proj, lb_logits, norm_w, layer_j):
    b, s, n4 = proj.shape
    hd = n4 // 4
    heads = hd // HEAD_DIM
    hg = 2 if heads % 2 == 0 else 1
    steps = heads // hg
    tc = _tile(s, 512)
    nl = lb_logits.shape[0]
    lbl = lb_logits.astype(F32).reshape(nl, heads, HEAD_DIM).transpose(1, 0, 2)
    leaf_shape = (hg * (tc // HG_CHUNK), HG_CHUNK // HG_LEAF, HG_LEAF, HEAD_DIM)

    def col(section):
        return pl.BlockSpec((1, tc, hg * HEAD_DIM), lambda i, h, t: (i, t, section * steps + h))

    return pl.pallas_call(
        functools.partial(_hgrn_kernel, layer_j=layer_j, hg=hg),
        grid=(b, steps, s // tc),
        in_specs=[col(0), col(1), col(2), col(3),
                  pl.BlockSpec((hg, nl, HEAD_DIM), lambda i, h, t: (h, 0, 0)),
                  pl.BlockSpec((1, HEAD_DIM), lambda i, h, t: (0, 0))],
        out_specs=pl.BlockSpec((1, tc, hg * HEAD_DIM), lambda i, h, t: (i, t, h)),
        out_shape=jax.ShapeDtypeStruct((b, s, hd), BF16),
        scratch_shapes=[pltpu.VMEM((hg, HEAD_DIM, HEAD_DIM), F32),
                        pltpu.VMEM(leaf_shape, F32),
                        pltpu.VMEM(leaf_shape, F32)],
        compiler_params=_cparams(("parallel", "parallel", "arbitrary")),
        name="hgrn2_recurrence",
    )(proj, proj, proj, proj, lbl, norm_w.astype(F32).reshape(1, HEAD_DIM))


def _l2n(x):
    return x * lax.rsqrt(jnp.sum(x * x, axis=-1, keepdims=True) + EPS)


def _gdn_gates_kernel(ba_ref, ga_ref, o_ref, *, v_heads):
    ba = ba_ref[0]
    t = ba.shape[0]
    beta = _sigmoid(ba)
    xa = ba + ga_ref[1:2, :]
    g = -jnp.exp(ga_ref[0:1, :]) * (jnp.maximum(xa, 0.0) + jnp.log1p(jnp.exp(-jnp.abs(xa))))
    row = lax.broadcasted_iota(jnp.int32, (t, t), 0)
    col = lax.broadcasted_iota(jnp.int32, (t, t), 1)
    block_tril = jnp.where((row >= col) & (row // GDN_CHUNK == col // GDN_CHUNK), 1.0, 0.0).astype(BF16)
    gc = _split_dot(block_tril, g)
    lane = lax.broadcasted_iota(jnp.int32, ba.shape, 1)
    o_ref[0] = jnp.where(lane < v_heads, beta, gc)


def gdn_gates(ba, a_log, dt_bias, v_heads):
    b, s, w = ba.shape
    tc = _tile(s, 512)
    pad = w - 2 * v_heads
    zeros = jnp.zeros((v_heads,), F32)
    gate_params = jnp.stack([
        jnp.concatenate([zeros, a_log.astype(F32), jnp.zeros((pad,), F32)]),
        jnp.concatenate([zeros, dt_bias.astype(F32), jnp.zeros((pad,), F32)]),
    ])
    return pl.pallas_call(
        functools.partial(_gdn_gates_kernel, v_heads=v_heads),
        grid=(b, s // tc),
        in_specs=[pl.BlockSpec((1, tc, w), lambda i, t: (i, t, 0)),
                  pl.BlockSpec((2, w), lambda i, t: (0, 0))],
        out_specs=pl.BlockSpec((1, tc, w), lambda i, t: (i, t, 0)),
        out_shape=jax.ShapeDtypeStruct((b, s, w), F32),
        compiler_params=_cparams(("parallel", "parallel")),
        name="gdn_gates",
    )(ba, gate_params)


def _conv_silu(xp_ref, w, lanes, nrows):
    out = xp_ref[8:8 + nrows, lanes] * w[GDN_CONV - 1:GDN_CONV, :]
    for back in range(1, GDN_CONV):
        out = out + xp_ref[pl.ds(8 - back, nrows), lanes] * w[GDN_CONV - 1 - back:GDN_CONV - back, :]
    return _silu(out)


def _gdn_kernel(q_ref, k_ref, v_ref, z_ref, gates_ref, cq_ref, ck_ref, cv_ref, nw_ref,
                o_ref, state_ref, xq_ref, xk_ref, xv_ref, gc_ref,
                w2_ref, n_ref, qp_ref, o0_ref, *, kg, rep, v_heads):
    tc = q_ref.shape[1]
    hd = HEAD_DIM
    c = GDN_CHUNK
    n_chunks = tc // c
    nh = kg * rep

    @pl.when(pl.program_id(2) == 0)
    def _():
        state_ref[...] = jnp.zeros_like(state_ref)
        xq_ref[tc:, :] = jnp.zeros((8, xq_ref.shape[1]), F32)
        xk_ref[tc:, :] = jnp.zeros((8, xk_ref.shape[1]), F32)
        xv_ref[tc:, :] = jnp.zeros((8, xv_ref.shape[1]), F32)

    row = lax.broadcasted_iota(jnp.int32, (c, c), 0)
    col = lax.broadcasted_iota(jnp.int32, (c, c), 1)
    eye = (row == col).astype(F32)
    rows = [slice(j * c, (j + 1) * c) for j in range(n_chunks)]

    for xp_ref, raw_ref in ((xq_ref, q_ref), (xk_ref, k_ref), (xv_ref, v_ref)):
        xp_ref[0:8, :] = xp_ref[tc:tc + 8, :]
        xp_ref[8:, :] = raw_ref[0].astype(F32)

    def prepare(khs):
        q, k, v, beta, gcb = {}, {}, {}, {}, {}
        gates = gates_ref[0]
        lane = lax.broadcasted_iota(jnp.int32, gates.shape, 1)
        for g in khs:
            lanes = slice(g * hd, (g + 1) * hd)
            qn = _l2n(_conv_silu(xq_ref, cq_ref[:, lanes], lanes, tc)) * (hd ** -0.5)
            kn = _l2n(_conv_silu(xk_ref, ck_ref[:, lanes], lanes, tc))
            for j in range(n_chunks):
                q[j, g] = qn[rows[j], :]
                k[j, g] = kn[rows[j], :]
            for m in range(g * rep, (g + 1) * rep):
                vlanes = slice(m * hd, (m + 1) * hd)
                va = _conv_silu(xv_ref, cv_ref[:, vlanes], vlanes, tc)
                vh = pl.program_id(1) * nh + m
                beta_col = jnp.sum(jnp.where(lane == vh, gates, 0.0), axis=-1, keepdims=True)
                gc_col = jnp.sum(jnp.where(lane == vh + v_heads, gates, 0.0), axis=-1, keepdims=True)
                beta_b = jnp.broadcast_to(beta_col, (tc, hd))
                gc_b = jnp.broadcast_to(gc_col, (tc, hd))
                gc_ref[m] = gc_b
                for j in range(n_chunks):
                    v[j, m] = va[rows[j], :]
                    beta[j, m] = beta_b[rows[j], :]
                    gcb[j, m] = gc_b[rows[j], :]
        return q, k, v, beta, gcb

    def solve(khs, prepared):
        q, k, v, beta, gcb = prepared
        pairs = [(j, m) for j in range(n_chunks) for g in khs for m in range(g * rep, (g + 1) * rep)]
        kpairs = [(j, g) for j in range(n_chunks) for g in khs]
        kb = {jg: k[jg].astype(BF16) for jg in kpairs}
        kk = {jg: _bdot_nt(kb[jg], kb[jg]) for jg in kpairs}
        qk = {jg: _bdot_nt(q[jg], kb[jg]) for jg in kpairs}
        decay = {jm: jnp.exp(jnp.where(row >= col, gcb[jm][:, 0:c] - jnp.transpose(gcb[jm])[0:1, :], NEG))
                 for jm in pairs}
        a = {(j, m): beta[j, m][:, 0:c] * kk[j, m // rep] * decay[j, m] for j, m in pairs}
        t_inv = {jm: eye - jnp.where((row // 2 == col // 2) & (row > col), a[jm], 0.0) for jm in pairs}
        ab = {jm: a[jm].astype(BF16) for jm in pairs}
        tb = {jm: t_inv[jm].astype(BF16) for jm in pairs}
        zero = jnp.zeros((c, c), BF16)
        node = 4
        while node <= c:
            links = (row // node == col // node) & (row % node >= node // 2) & (col % node < node // 2)
            et = {jm: jnp.dot(jnp.where(links, ab[jm], zero), tb[jm], preferred_element_type=F32).astype(BF16)
                  for jm in pairs}
            t_inv = {jm: t_inv[jm] - jnp.dot(tb[jm], et[jm], preferred_element_type=F32) for jm in pairs}
            tb = {jm: t_inv[jm].astype(BF16) for jm in pairs}
            node *= 2
        e_gc = {jm: jnp.exp(gcb[jm]) for jm in pairs}
        ub = {jm: _bdot(tb[jm], v[jm] * beta[jm]).astype(BF16) for jm in pairs}
        wb = {(j, m): _bdot(tb[j, m], k[j, m // rep] * (beta[j, m] * e_gc[j, m])).astype(BF16)
              for j, m in pairs}
        attn = {(j, m): (qk[j, m // rep] * decay[j, m]).astype(BF16) for j, m in pairs}
        kdt = {(j, m): jnp.transpose(k[j, m // rep] * jnp.exp(gcb[j, m][c - 1:c, :] - gcb[j, m])).astype(BF16)
               for j, m in pairs}
        for j, m in pairs:
            w2_ref[m, j] = jnp.dot(kdt[j, m], wb[j, m], preferred_element_type=F32).astype(BF16)
        for j, m in pairs:
            n_ref[m, j] = jnp.dot(kdt[j, m], ub[j, m], preferred_element_type=F32)
        for j, m in pairs:
            qp_ref[m, rows[j], :] = (q[j, m // rep] * e_gc[j, m]
                                     - jnp.dot(attn[j, m], wb[j, m], preferred_element_type=F32)).astype(BF16)
        for j, m in pairs:
            o0_ref[m, rows[j], :] = jnp.dot(attn[j, m], ub[j, m], preferred_element_type=F32)

    half = max(kg // 2, 1)
    groups = [range(g0, min(g0 + half, kg)) for g0 in range(0, kg, half)]
    prepared = [prepare(khs) for khs in groups]
    for khs, prep in zip(groups, prepared):
        solve(khs, prep)

    heads = range(nh)
    nw = nw_ref[...]

    def advance(i):
        sl = pl.ds(pl.multiple_of(i * c, c), c)
        state = [state_ref[m] for m in heads]
        sb = [x.astype(BF16) for x in state]
        ws = [jnp.dot(w2_ref[m, i], sb[m], preferred_element_type=F32) for m in heads]
        o = [jnp.dot(qp_ref[m, sl, :], sb[m], preferred_element_type=F32) for m in heads]
        for m in heads:
            g_last = gc_ref[m, pl.ds(i * c + c - 1, 1), :]
            state_ref[m] = state[m] * jnp.exp(g_last) - ws[m] + n_ref[m, i]
        for m in heads:
            o0_ref[m, sl, :] += o[m]

    def finish(i):
        sl = pl.ds(pl.multiple_of(i * c, c), c)
        for m in heads:
            lanes = slice(m * hd, (m + 1) * hd)
            z = z_ref[0, sl, lanes].astype(F32)
            o_ref[0, sl, lanes] = (_rms(o0_ref[m, sl, :], nw) * _silu(z)).astype(o_ref.dtype)

    advance(0)

    def body(i, carry):
        finish(i - 1)
        advance(i)
        return carry

    lax.fori_loop(1, n_chunks, body, 0)
    finish(n_chunks - 1)


def gdn_recurrence(proj, gates, conv_w, norm_w, k_heads, v_heads):
    b, s, _ = proj.shape
    rep = v_heads // k_heads
    kg = 4 if k_heads % 4 == 0 else 1
    nh = kg * rep
    qw = kg * HEAD_DIM
    vw = nh * HEAD_DIM
    tc = _tile(s, 512)
    c = GDN_CHUNK
    n_chunks = tc // c
    steps = k_heads // kg
    v0 = 2 * k_heads * HEAD_DIM // vw
    z0 = v0 + v_heads * HEAD_DIM // vw
    conv_w = conv_w.astype(F32)

    return pl.pallas_call(
        functools.partial(_gdn_kernel, kg=kg, rep=rep, v_heads=v_heads),
        grid=(b, steps, s // tc),
        in_specs=[pl.BlockSpec((1, tc, qw), lambda i, h, t: (i, t, h)),
                  pl.BlockSpec((1, tc, qw), lambda i, h, t: (i, t, steps + h)),
                  pl.BlockSpec((1, tc, vw), lambda i, h, t: (i, t, v0 + h)),
                  pl.BlockSpec((1, tc, vw), lambda i, h, t: (i, t, z0 + h)),
                  pl.BlockSpec((1, tc, HEAD_DIM), lambda i, h, t: (i, t, 0)),
                  pl.BlockSpec((GDN_CONV, qw), lambda i, h, t: (0, h)),
                  pl.BlockSpec((GDN_CONV, qw), lambda i, h, t: (0, steps + h)),
                  pl.BlockSpec((GDN_CONV, vw), lambda i, h, t: (0, v0 + h)),
                  pl.BlockSpec((1, HEAD_DIM), lambda i, h, t: (0, 0))],
        out_specs=pl.BlockSpec((1, tc, vw), lambda i, h, t: (i, t, h)),
        out_shape=jax.ShapeDtypeStruct((b, s, v_heads * HEAD_DIM), BF16),
        scratch_shapes=[pltpu.VMEM((nh, HEAD_DIM, HEAD_DIM), F32),
                        pltpu.VMEM((tc + 8, qw), F32),
                        pltpu.VMEM((tc + 8, qw), F32),
                        pltpu.VMEM((tc + 8, vw), F32),
                        pltpu.VMEM((nh, tc, HEAD_DIM), F32),
                        pltpu.VMEM((nh, n_chunks, HEAD_DIM, HEAD_DIM), BF16),
                        pltpu.VMEM((nh, n_chunks, HEAD_DIM, HEAD_DIM), F32),
                        pltpu.VMEM((nh, tc, HEAD_DIM), BF16),
                        pltpu.VMEM((nh, tc, HEAD_DIM), F32)],
        compiler_params=_cparams(("parallel", "parallel", "arbitrary")),
        name="gdn_recurrence",
    )(proj, proj, proj, proj, gates, conv_w, conv_w, conv_w, norm_w.astype(F32).reshape(1, HEAD_DIM))


def kernel(x, c, ada_w, ada_b, norm_w, hg_w_in, hg_lb_logits, hg_norm_w, hg_w_out, gdn_w_in,
           gdn_conv_w, gdn_A_log, gdn_dt_bias, gdn_norm_w, gdn_w_out, ffn_w_gate_up, ffn_w_down):
    depth = ada_w.shape[0]
    b, s, d = x.shape
    v_heads = gdn_A_log.shape[1]
    k_heads = v_heads // 2
    gdn_main = 2 * k_heads * HEAD_DIM + 2 * v_heads * HEAD_DIM

    mod_all = ada_modulation(c, ada_w, ada_b).reshape(depth, b, N_MOD, d)
    for layer in range(depth):
        mod = mod_all[layer]
        nw = norm_w[layer]
        j = layer // 2
        if layer % 2 == 0:
            proj = norm_mod_project(x, nw[0:1], mod, hg_w_in[j].astype(BF16), 0, F32)
            mixed = hgrn2_recurrence(proj, hg_lb_logits, hg_norm_w[j], j)
            w_out = hg_w_out[j]
        else:
            w_in = gdn_w_in[j]
            w_tail = jnp.pad(w_in[:, gdn_main:], ((0, 0), (0, HEAD_DIM - 2 * v_heads)))
            proj, ba = norm_mod_project(x, nw[0:1], mod, w_in.astype(BF16), 0, BF16, n=gdn_main,
                                        w_tail=w_tail.astype(BF16))
            gates = gdn_gates(ba, gdn_A_log[j], gdn_dt_bias[j], v_heads)
            mixed = gdn_recurrence(proj, gates, gdn_conv_w[j], gdn_norm_w[j], k_heads, v_heads)
            w_out = gdn_w_out[j]
        x = out_project_residual(mixed, w_out.astype(BF16), x, nw[1:2], mod, 2)
        x = ffn_residual(x, nw[2:3], nw[3:4], mod, ffn_w_gate_up[layer].astype(BF16),
                         ffn_w_down[layer].astype(BF16))
    return x
```

```python
import functools

import jax
import jax.numpy as jnp
from jax import lax
from jax.experimental import pallas as pl
from jax.experimental.pallas import tpu as pltpu

F32 = jnp.float32
BF16 = jnp.bfloat16
EPS = 1e-6
N_MOD = 6
HEAD_DIM = 128
GDN_CONV = 4
SUBLANES = 8
GDN_CHUNK = 64
HG_CHUNK = 128
HG_LEAF = 8
NEG = -1e30
LOG2E = 1.4426950408889634
ROWS_PER_PASS = 256
VMEM_LIMIT = 56 * 1024 * 1024


def _cparams(sem):
    return pltpu.CompilerParams(dimension_semantics=sem, vmem_limit_bytes=VMEM_LIMIT)


def _tile(n, pref):
    t = min(n, pref)
    while n % t:
        t //= 2
    return t


def _sigmoid(x):
    return 0.5 * jnp.tanh(0.5 * x) + 0.5


def _silu(x):
    h = 0.5 * x
    return h * jnp.tanh(h) + h


def _bdot(a, b):
    return jnp.dot(a.astype(BF16), b.astype(BF16), preferred_element_type=F32)


def _bdot_nt(a, b):
    return lax.dot_general(a.astype(BF16), b.astype(BF16), (((1,), (1,)), ((), ())),
                           preferred_element_type=F32)


def _bdot_tn(a, b):
    return lax.dot_general(a.astype(BF16), b.astype(BF16), (((0,), (0,)), ((), ())),
                           preferred_element_type=F32)


def _split_dot(a_exact_bf16, x):
    hi = x.astype(BF16)
    lo = (x - hi.astype(F32)).astype(BF16)
    return (jnp.dot(a_exact_bf16, hi, preferred_element_type=F32)
            + jnp.dot(a_exact_bf16, lo, preferred_element_type=F32))


def _rms(x, w):
    ms = jnp.mean(x * x, axis=-1, keepdims=True)
    return x * lax.rsqrt(ms + EPS) * w


def _ada_kernel(c_ref, w_ref, b_ref, o_ref):
    c = c_ref[...]
    o_ref[0] = _bdot(_silu(c), w_ref[0]) + b_ref[0]


def ada_modulation(c, ada_w, ada_b):
    depth, d, n = ada_w.shape
    b = c.shape[0]
    tn = _tile(n, 2048)
    return pl.pallas_call(
        _ada_kernel,
        grid=(depth, n // tn),
        in_specs=[pl.BlockSpec((b, d), lambda l, j: (0, 0)),
                  pl.BlockSpec((1, d, tn), lambda l, j: (l, 0, j)),
                  pl.BlockSpec((1, 1, tn), lambda l, j: (l, 0, j))],
        out_specs=pl.BlockSpec((1, b, tn), lambda l, j: (l, 0, j)),
        out_shape=jax.ShapeDtypeStruct((depth, b, n), F32),
        compiler_params=_cparams(("parallel", "parallel")),
        name="ada_modulation",
    )(c, ada_w, ada_b.reshape(depth, 1, n))


def _row_passes(rows, per=ROWS_PER_PASS):
    per = _tile(rows, per)
    return [slice(r * per, (r + 1) * per) for r in range(rows // per)]


def _inv_rms(load):
    x = load()
    return lax.rsqrt(jnp.mean(x * x, axis=-1, keepdims=True) + EPS)


def _norm_mod_rows(x_ref, rows, nw, mod_ref, shift_row):
    gain = nw * (1.0 + mod_ref[0, shift_row + 1:shift_row + 2, :])
    shift = mod_ref[0, shift_row:shift_row + 1, :]
    load = lambda: x_ref[0, rows, :]
    return load() * _inv_rms(load) * gain + shift


def _proj_kernel(x_ref, nw_ref, mod_ref, w_ref, *rest, shift_row, with_tail):
    if with_tail:
        wt_ref, o_ref, ot_ref, h_ref = rest
    else:
        o_ref, h_ref = rest
    tm = h_ref.shape[0]

    def body(first):
        for rows in _row_passes(tm, 512):
            if first:
                for sub in _row_passes(rows.stop - rows.start):
                    rr = slice(rows.start + sub.start, rows.start + sub.stop)
                    h_ref[rr, :] = _norm_mod_rows(x_ref, rr, nw_ref[...], mod_ref, shift_row).astype(BF16)
            o_ref[0, rows, :] = jnp.dot(h_ref[rows, :], w_ref[...],
                                        preferred_element_type=F32).astype(o_ref.dtype)
            if first and with_tail:
                ot_ref[0, rows, :] = jnp.dot(h_ref[rows, :], wt_ref[...], preferred_element_type=F32)

    @pl.when(pl.program_id(2) == 0)
    def _():
        body(True)

    @pl.when(pl.program_id(2) > 0)
    def _():
        body(False)


def norm_mod_project(x, nw, mod, w, shift_row, out_dtype, n=None, w_tail=None):
    b, s, d = x.shape
    n = w.shape[1] if n is None else n
    tm = _tile(s, 1024)
    tn = _tile(n, 1024)
    in_specs = [pl.BlockSpec((1, tm, d), lambda i, m, j: (i, m, 0)),
                pl.BlockSpec((1, d), lambda i, m, j: (0, 0)),
                pl.BlockSpec((1, N_MOD, d), lambda i, m, j: (i, 0, 0)),
                pl.BlockSpec((d, tn), lambda i, m, j: (0, j))]
    out_specs = pl.BlockSpec((1, tm, tn), lambda i, m, j: (i, m, j))
    out_shape = jax.ShapeDtypeStruct((b, s, n), out_dtype)
    args = (x, nw, mod, w)
    if w_tail is not None:
        nt = w_tail.shape[1]
        in_specs.append(pl.BlockSpec((d, nt), lambda i, m, j: (0, 0)))
        out_specs = [out_specs, pl.BlockSpec((1, tm, nt), lambda i, m, j: (i, m, 0))]
        out_shape = [out_shape, jax.ShapeDtypeStruct((b, s, nt), F32)]
        args = args + (w_tail,)
    return pl.pallas_call(
        functools.partial(_proj_kernel, shift_row=shift_row, with_tail=w_tail is not None),
        grid=(b, s // tm, n // tn),
        in_specs=in_specs,
        out_specs=out_specs,
        out_shape=out_shape,
        scratch_shapes=[pltpu.VMEM((tm, d), BF16)],
        compiler_params=_cparams(("parallel", "parallel", "arbitrary")),
        name="norm_mod_project",
    )(*args)


def _out_kernel(a_ref, w_ref, x_ref, nw_ref, mod_ref, o_ref, *, gate_row):
    o_ref[0] = jnp.dot(a_ref[0], w_ref[...], preferred_element_type=F32)
    gain = mod_ref[0, gate_row:gate_row + 1, :] * nw_ref[...]
    for rows in _row_passes(o_ref.shape[1]):
        load = lambda: o_ref[0, rows, :]
        o_ref[0, rows, :] = x_ref[0, rows, :] + load() * _inv_rms(load) * gain


def out_project_residual(a, w, x, nw, mod, gate_row):
    b, s, k = a.shape
    d = w.shape[1]
    tm = _tile(s, 512)
    return pl.pallas_call(
        functools.partial(_out_kernel, gate_row=gate_row),
        grid=(b, s // tm),
        in_specs=[pl.BlockSpec((1, tm, k), lambda i, m: (i, m, 0)),
                  pl.BlockSpec((k, d), lambda i, m: (0, 0), pipeline_mode=pl.Buffered(1)),
                  pl.BlockSpec((1, tm, d), lambda i, m: (i, m, 0)),
                  pl.BlockSpec((1, d), lambda i, m: (0, 0)),
                  pl.BlockSpec((1, N_MOD, d), lambda i, m: (i, 0, 0))],
        out_specs=pl.BlockSpec((1, tm, d), lambda i, m: (i, m, 0)),
        out_shape=jax.ShapeDtypeStruct((b, s, d), F32),
        compiler_params=_cparams(("parallel", "parallel")),
        name="out_project_residual",
    )(a, w, x, nw, mod)


def _ffn_kernel(x_ref, nw_in_ref, mod_ref, wg_ref, wu_ref, wd_ref, nw_out_ref, o_ref, h_ref):
    k = pl.program_id(2)
    nk = pl.num_programs(2)
    tm = h_ref.shape[0]

    def body(first, last):
        for rows in _row_passes(tm, 512):
            if first:
                for sub in _row_passes(rows.stop - rows.start):
                    rr = slice(rows.start + sub.start, rows.start + sub.stop)
                    h_ref[rr, :] = _norm_mod_rows(x_ref, rr, nw_in_ref[...], mod_ref, 3).astype(BF16)
            h = h_ref[rows, :]
            g = jnp.dot(h, wg_ref[...], preferred_element_type=F32)
            u = jnp.dot(h, wu_ref[...], preferred_element_type=F32)
            part = jnp.dot((_silu(g) * u).astype(BF16), wd_ref[...], preferred_element_type=F32)
            if first:
                o_ref[0, rows, :] = part
            else:
                o_ref[0, rows, :] += part
            if last:
                gain = mod_ref[0, 5:6, :] * nw_out_ref[...]
                for sub in _row_passes(rows.stop - rows.start):
                    rr = slice(rows.start + sub.start, rows.start + sub.stop)
                    load = lambda: o_ref[0, rr, :]
                    o_ref[0, rr, :] = x_ref[0, rr, :] + load() * _inv_rms(load) * gain

    @pl.when((k == 0) & (nk > 1))
    def _():
        body(True, False)

    @pl.when((k > 0) & (k < nk - 1))
    def _():
        body(False, False)

    @pl.when((k == nk - 1) & (nk > 1))
    def _():
        body(False, True)

    @pl.when(nk == 1)
    def _():
        body(True, True)


def ffn_residual(x, nw_in, nw_out, mod, w_gate_up, w_down):
    b, s, d = x.shape
    f = w_down.shape[0]
    tm = _tile(s, 1024)
    tf = _tile(f, 512)
    nf = f // tf
    return pl.pallas_call(
        _ffn_kernel,
        grid=(b, s // tm, nf),
        in_specs=[pl.BlockSpec((1, tm, d), lambda i, m, k: (i, m, 0)),
                  pl.BlockSpec((1, d), lambda i, m, k: (0, 0)),
                  pl.BlockSpec((1, N_MOD, d), lambda i, m, k: (i, 0, 0)),
                  pl.BlockSpec((d, tf), lambda i, m, k: (0, k)),
                  pl.BlockSpec((d, tf), lambda i, m, k: (0, k + nf)),
                  pl.BlockSpec((tf, d), lambda i, m, k: (k, 0)),
                  pl.BlockSpec((1, d), lambda i, m, k: (0, 0))],
        out_specs=pl.BlockSpec((1, tm, d), lambda i, m, k: (i, m, 0)),
        out_shape=jax.ShapeDtypeStruct((b, s, d), F32),
        scratch_shapes=[pltpu.VMEM((tm, d), BF16)],
        compiler_params=_cparams(("parallel", "parallel", "arbitrary")),
        name="ffn_residual",
    )(x, nw_in, mod, w_gate_up, w_gate_up, w_down, nw_out)


def _row_group_bcast(x, group, row):
    c, w = x.shape
    x3 = x.reshape(c // group, group, w)
    return jnp.broadcast_to(x3[:, row:row + 1, :], x3.shape).reshape(c, w)


def _hgrn_kernel(q_ref, f_ref, v_ref, g_ref, lbl_ref, nw_ref, o_ref, state_ref, kleaf_ref, bleaf_ref,
                 *, layer_j, hg):
    c = HG_CHUNK
    hd = HEAD_DIM
    n_chunks = q_ref.shape[1] // c
    pairs = [(j, m) for j in range(n_chunks) for m in range(hg)]

    @pl.when(pl.program_id(2) == 0)
    def _():
        state_ref[...] = jnp.zeros_like(state_ref)

    log_lb, log_1mlb = [], []
    for m in range(hg):
        logits = lbl_ref[m]
        e = jnp.exp(logits - jnp.max(logits, axis=0, keepdims=True))
        sm = e / jnp.sum(e, axis=0, keepdims=True)
        lb = jnp.sum(sm[:layer_j + 1], axis=0, keepdims=True) - sm[0:1]
        log_lb.append(jnp.log(lb))
        log_1mlb.append(jnp.log1p(-lb))

    row = lax.broadcasted_iota(jnp.int32, (c, c), 0)
    col = lax.broadcasted_iota(jnp.int32, (c, c), 1)
    rowk = lax.broadcasted_iota(jnp.int32, (c, hd), 0)
    tril = (row >= col).astype(BF16)

    def blk(ref, j, m):
        return ref[0, j * c:(j + 1) * c, m * hd:(m + 1) * hd]

    q = {(j, m): _silu(blk(q_ref, j, m)) for j, m in pairs}
    v = {(j, m): blk(v_ref, j, m) for j, m in pairs}
    log_f = {}
    for j, m in pairs:
        f_raw = blk(f_ref, j, m)
        log_sig = jnp.minimum(f_raw, 0.0) - jnp.log(1.0 + jnp.exp(-jnp.abs(f_raw)))
        cand = log_1mlb[m] + log_sig
        log_f[j, m] = jnp.maximum(log_lb[m], cand) + jnp.log(1.0 + jnp.exp(-jnp.abs(log_lb[m] - cand)))
    k = {jm: 1.0 - jnp.exp(log_f[jm]) for jm in pairs}
    b = {jm: _split_dot(tril, log_f[jm]) * LOG2E for jm in pairs}
    b_last = {jm: b[jm][c - 1:c, :] for jm in pairs}

    scores = {jm: jnp.zeros((c, c), F32) for jm in pairs}
    n = c
    while n > HG_LEAF:
        half = n // 2
        right = (rowk % n) >= half
        same = (row // n) == (col // n)
        qt, kt = {}, {}
        for jm in pairs:
            r = _row_group_bcast(b[jm], n, half - 1)
            qt[jm] = q[jm] * jnp.exp2(jnp.where(right, b[jm] - r, NEG))
            kt[jm] = k[jm] * jnp.exp2(jnp.where(right, NEG, r - b[jm]))
        scores = {jm: scores[jm] + jnp.where(same, _bdot_nt(qt[jm], kt[jm]), 0.0) for jm in pairs}
        n = half

    g = c // HG_LEAF
    t_in = lax.broadcasted_iota(jnp.int32, (g, HG_LEAF, hd), 1)
    col3 = lax.broadcasted_iota(jnp.int32, (g, HG_LEAF, c), 2)
    blk0 = lax.broadcasted_iota(jnp.int32, (g, HG_LEAF, c), 0) * HG_LEAF
    for p_idx, jm in enumerate(pairs):
        q3 = q[jm].reshape(g, HG_LEAF, hd)
        b3 = b[jm].reshape(g, HG_LEAF, hd)
        s3 = scores[jm].reshape(g, HG_LEAF, c)
        kleaf_ref[p_idx] = k[jm].reshape(g, HG_LEAF, hd)
        bleaf_ref[p_idx] = b3
        for s in range(HG_LEAF):
            ks = kleaf_ref[p_idx, :, pl.ds(s, 1), :]
            bs = bleaf_ref[p_idx, :, pl.ds(s, 1), :]
            p = q3 * ks * jnp.exp2(jnp.where(t_in >= s, b3 - bs, NEG))
            s3 = jnp.where(col3 == blk0 + s, jnp.sum(p, axis=-1, keepdims=True), s3)
        scores[jm] = s3.reshape(c, c)
    o = {jm: _bdot(scores[jm], v[jm]) for jm in pairs}

    qe = {jm: (q[jm] * jnp.exp2(b[jm])).astype(BF16) for jm in pairs}
    upd = {jm: _bdot_tn(v[jm], k[jm] * jnp.exp2(b_last[jm] - b[jm])) for jm in pairs}

    state = [state_ref[m] for m in range(hg)]
    for j in range(n_chunks):
        for m in range(hg):
            o[j, m] = o[j, m] + _bdot_nt(qe[j, m], state[m])
            state[m] = state[m] * jnp.exp2(b_last[j, m]) + upd[j, m]
    for m in range(hg):
        state_ref[m] = state[m]

    nw = nw_ref[...]
    for j, m in pairs:
        gate = _sigmoid(blk(g_ref, j, m))
        o_ref[0, j * c:(j + 1) * c, m * hd:(m + 1) * hd] = (_rms(o[j, m], nw) * gate).astype(o_ref.dtype)


def hgrn2_recurrence(proj, lb_logits, norm_w, layer_j):
    b, s, n4 = proj.shape
    hd = n4 // 4
    heads = hd // HEAD_DIM
    hg = 2 if heads % 2 == 0 else 1
    steps = heads // hg
    tc = _tile(s, 512)
    nl = lb_logits.shape[0]
    lbl = lb_logits.astype(F32).reshape(nl, heads, HEAD_DIM).transpose(1, 0, 2)
    leaf_shape = (hg * (tc // HG_CHUNK), HG_CHUNK // HG_LEAF, HG_LEAF, HEAD_DIM)

    def col(section):
        return pl.BlockSpec((1, tc, hg * HEAD_DIM), lambda i, h, t: (i, t, section * steps + h))

    return pl.pallas_call(
        functools.partial(_hgrn_kernel, layer_j=layer_j, hg=hg),
        grid=(b, steps, s // tc),
        in_specs=[col(0), col(1), col(2), col(3),
                  pl.BlockSpec((hg, nl, HEAD_DIM), lambda i, h, t: (h, 0, 0)),
                  pl.BlockSpec((1, HEAD_DIM), lambda i, h, t: (0, 0))],
        out_specs=pl.BlockSpec((1, tc, hg * HEAD_DIM), lambda i, h, t: (i, t, h)),
        out_shape=jax.ShapeDtypeStruct((b, s, hd), BF16),
        scratch_shapes=[pltpu.VMEM((hg, HEAD_DIM, HEAD_DIM), F32),
                        pltpu.VMEM(leaf_shape, F32),
                        pltpu.VMEM(leaf_shape, F32)],
        compiler_params=_cparams(("parallel", "parallel", "arbitrary")),
        name="hgrn2_recurrence",
    )(proj, proj, proj, proj, lbl, norm_w.astype(F32).reshape(1, HEAD_DIM))


def _l2n(x):
    return x * lax.rsqrt(jnp.sum(x * x, axis=-1, keepdims=True) + EPS)


def _gdn_gates_kernel(ba_ref, ga_ref, o_ref, *, v_heads):
    ba = ba_ref[0]
    t = ba.shape[0]
    beta = _sigmoid(ba)
    xa = ba + ga_ref[1:2, :]
    g = -jnp.exp(ga_ref[0:1, :]) * (jnp.maximum(xa, 0.0) + jnp.log1p(jnp.exp(-jnp.abs(xa))))
    row = lax.broadcasted_iota(jnp.int32, (t, t), 0)
    col = lax.broadcasted_iota(jnp.int32, (t, t), 1)
    block_tril = jnp.where((row >= col) & (row // GDN_CHUNK == col // GDN_CHUNK), 1.0, 0.0).astype(BF16)
    gc = _split_dot(block_tril, g)
    lane = lax.broadcasted_iota(jnp.int32, ba.shape, 1)
    o_ref[0] = jnp.where(lane < v_heads, beta, gc)


def gdn_gates(ba, a_log, dt_bias, v_heads):
    b, s, w = ba.shape
    tc = _tile(s, 512)
    pad = w - 2 * v_heads
    zeros = jnp.zeros((v_heads,), F32)
    gate_params = jnp.stack([
        jnp.concatenate([zeros, a_log.astype(F32), jnp.zeros((pad,), F32)]),
        jnp.concatenate([zeros, dt_bias.astype(F32), jnp.zeros((pad,), F32)]),
    ])
    return pl.pallas_call(
        functools.partial(_gdn_gates_kernel, v_heads=v_heads),
        grid=(b, s // tc),
        in_specs=[pl.BlockSpec((1, tc, w), lambda i, t: (i, t, 0)),
                  pl.BlockSpec((2, w), lambda i, t: (0, 0))],
        out_specs=pl.BlockSpec((1, tc, w), lambda i, t: (i, t, 0)),
        out_shape=jax.ShapeDtypeStruct((b, s, w), F32),
        compiler_params=_cparams(("parallel", "parallel")),
        name="gdn_gates",
    )(ba, gate_params)


def _conv_silu(xp_ref, w, lanes, nrows):
    out = xp_ref[SUBLANES:SUBLANES + nrows, lanes] * w[GDN_CONV - 1:GDN_CONV, :]
    for back in range(1, GDN_CONV):
        out = out + xp_ref[pl.ds(SUBLANES - back, nrows), lanes] * w[GDN_CONV - 1 - back:GDN_CONV - back, :]
    return _silu(out)


def _gdn_kernel(q_ref, k_ref, v_ref, z_ref, gates_ref, cq_ref, ck_ref, cv_ref, nw_ref,
                o_ref, state_ref, xq_ref, xk_ref, xv_ref, gc_ref,
                w2_ref, n_ref, qp_ref, o0_ref, *, kg, rep, v_heads):
    tc = q_ref.shape[1]
    hd = HEAD_DIM
    c = GDN_CHUNK
    n_chunks = tc // c
    nh = kg * rep

    @pl.when(pl.program_id(2) == 0)
    def _():
        state_ref[...] = jnp.zeros_like(state_ref)
        xq_ref[tc:, :] = jnp.zeros((SUBLANES, xq_ref.shape[1]), F32)
        xk_ref[tc:, :] = jnp.zeros((SUBLANES, xk_ref.shape[1]), F32)
        xv_ref[tc:, :] = jnp.zeros((SUBLANES, xv_ref.shape[1]), F32)

    row = lax.broadcasted_iota(jnp.int32, (c, c), 0)
    col = lax.broadcasted_iota(jnp.int32, (c, c), 1)
    eye = (row == col).astype(F32)
    rows = [slice(j * c, (j + 1) * c) for j in range(n_chunks)]

    for xp_ref, raw_ref in ((xq_ref, q_ref), (xk_ref, k_ref), (xv_ref, v_ref)):
        xp_ref[0:SUBLANES, :] = xp_ref[tc:tc + SUBLANES, :]
        xp_ref[SUBLANES:, :] = raw_ref[0].astype(F32)

    def prepare(khs):
        q, k, v, beta, gcb = {}, {}, {}, {}, {}
        gates = gates_ref[0]
        lane = lax.broadcasted_iota(jnp.int32, gates.shape, 1)
        for g in khs:
            lanes = slice(g * hd, (g + 1) * hd)
            qn = _l2n(_conv_silu(xq_ref, cq_ref[:, lanes], lanes, tc)) * (hd ** -0.5)
            kn = _l2n(_conv_silu(xk_ref, ck_ref[:, lanes], lanes, tc))
            for j in range(n_chunks):
                q[j, g] = qn[rows[j], :]
                k[j, g] = kn[rows[j], :]
            for m in range(g * rep, (g + 1) * rep):
                vlanes = slice(m * hd, (m + 1) * hd)
                va = _conv_silu(xv_ref, cv_ref[:, vlanes], vlanes, tc)
                vh = pl.program_id(1) * nh + m
                beta_col = jnp.sum(jnp.where(lane == vh, gates, 0.0), axis=-1, keepdims=True)
                gc_col = jnp.sum(jnp.where(lane == vh + v_heads, gates, 0.0), axis=-1, keepdims=True)
                beta_b = jnp.broadcast_to(beta_col, (tc, hd))
                gc_b = jnp.broadcast_to(gc_col, (tc, hd))
                gc_ref[m] = gc_b
                for j in range(n_chunks):
                    v[j, m] = va[rows[j], :]
                    beta[j, m] = beta_b[rows[j], :]
                    gcb[j, m] = gc_b[rows[j], :]
        return q, k, v, beta, gcb

    def solve(khs, prepared):
        q, k, v, beta, gcb = prepared
        pairs = [(j, m) for j in range(n_chunks) for g in khs for m in range(g * rep, (g + 1) * rep)]
        kpairs = [(j, g) for j in range(n_chunks) for g in khs]
        kb = {jg: k[jg].astype(BF16) for jg in kpairs}
        kk = {jg: _bdot_nt(kb[jg], kb[jg]) for jg in kpairs}
        qk = {jg: _bdot_nt(q[jg], kb[jg]) for jg in kpairs}
        decay = {jm: jnp.exp(jnp.where(row >= col, gcb[jm][:, 0:c] - jnp.transpose(gcb[jm])[0:1, :], NEG))
                 for jm in pairs}
        a = {(j, m): beta[j, m][:, 0:c] * kk[j, m // rep] * decay[j, m] for j, m in pairs}
        t_inv = {jm: eye - jnp.where((row // 2 == col // 2) & (row > col), a[jm], 0.0) for jm in pairs}
        ab = {jm: a[jm].astype(BF16) for jm in pairs}
        tb = {jm: t_inv[jm].astype(BF16) for jm in pairs}
        zero = jnp.zeros((c, c), BF16)
        node = 4
        while node <= c:
            links = (row // node == col // node) & (row % node >= node // 2) & (col % node < node // 2)
            et = {jm: jnp.dot(jnp.where(links, ab[jm], zero), tb[jm], preferred_element_type=F32).astype(BF16)
                  for jm in pairs}
            t_inv = {jm: t_inv[jm] - jnp.dot(tb[jm], et[jm], preferred_element_type=F32) for jm in pairs}
            tb = {jm: t_inv[jm].astype(BF16) for jm in pairs}
            node *= 2
        e_gc = {jm: jnp.exp(gcb[jm]) for jm in pairs}
        ub = {jm: _bdot(tb[jm], v[jm] * beta[jm]).astype(BF16) for jm in pairs}
        wb = {(j, m): _bdot(tb[j, m], k[j, m // rep] * (beta[j, m] * e_gc[j, m])).astype(BF16)
              for j, m in pairs}
        attn = {(j, m): (qk[j, m // rep] * decay[j, m]).astype(BF16) for j, m in pairs}
        kdt = {(j, m): jnp.transpose(k[j, m // rep] * jnp.exp(gcb[j, m][c - 1:c, :] - gcb[j, m])).astype(BF16)
               for j, m in pairs}
        for j, m in pairs:
            w2_ref[m, j] = jnp.dot(kdt[j, m], wb[j, m], preferred_element_type=F32).astype(BF16)
        for j, m in pairs:
            n_ref[m, j] = jnp.dot(kdt[j, m], ub[j, m], preferred_element_type=F32)
        for j, m in pairs:
            qp_ref[m, rows[j], :] = (q[j, m // rep] * e_gc[j, m]
                                     - jnp.dot(attn[j, m], wb[j, m], preferred_element_type=F32)).astype(BF16)
        for j, m in pairs:
            o0_ref[m, rows[j], :] = jnp.dot(attn[j, m], ub[j, m], preferred_element_type=F32)

    half = max(kg // 2, 1)
    groups = [range(g0, min(g0 + half, kg)) for g0 in range(0, kg, half)]
    prepared = [prepare(khs) for khs in groups]
    for khs, prep in zip(groups, prepared):
        solve(khs, prep)

    heads = range(nh)
    nw = nw_ref[...]

    def advance(i):
        sl = pl.ds(pl.multiple_of(i * c, c), c)
        state = [state_ref[m] for m in heads]
        sb = [x.astype(BF16) for x in state]
        ws = [jnp.dot(w2_ref[m, i], sb[m], preferred_element_type=F32) for m in heads]
        o = [jnp.dot(qp_ref[m, sl, :], sb[m], preferred_element_type=F32) for m in heads]
        for m in heads:
            g_last = gc_ref[m, pl.ds(i * c + c - 1, 1), :]
            state_ref[m] = state[m] * jnp.exp(g_last) - ws[m] + n_ref[m, i]
        for m in heads:
            o0_ref[m, sl, :] += o[m]

    def finish(i):
        sl = pl.ds(pl.multiple_of(i * c, c), c)
        for m in heads:
            lanes = slice(m * hd, (m + 1) * hd)
            z = z_ref[0, sl, lanes].astype(F32)
            o_ref[0, sl, lanes] = (_rms(o0_ref[m, sl, :], nw) * _silu(z)).astype(o_ref.dtype)

    advance(0)

    def body(i, carry):
        finish(i - 1)
        advance(i)
        return carry

    lax.fori_loop(1, n_chunks, body, 0, unroll=True)
    finish(n_chunks - 1)


def gdn_recurrence(proj, gates, conv_w, norm_w, k_heads, v_heads):
    b, s, _ = proj.shape
    rep = v_heads // k_heads
    kg = 4 if k_heads % 4 == 0 else 1
    nh = kg * rep
    qw = kg * HEAD_DIM
    vw = nh * HEAD_DIM
    tc = _tile(s, 512)
    c = GDN_CHUNK
    n_chunks = tc // c
    steps = k_heads // kg
    v0 = 2 * k_heads * HEAD_DIM // vw
    z0 = v0 + v_heads * HEAD_DIM // vw
    conv_w = conv_w.astype(F32)

    return pl.pallas_call(
        functools.partial(_gdn_kernel, kg=kg, rep=rep, v_heads=v_heads),
        grid=(b, steps, s // tc),
        in_specs=[pl.BlockSpec((1, tc, qw), lambda i, h, t: (i, t, h)),
                  pl.BlockSpec((1, tc, qw), lambda i, h, t: (i, t, steps + h)),
                  pl.BlockSpec((1, tc, vw), lambda i, h, t: (i, t, v0 + h)),
                  pl.BlockSpec((1, tc, vw), lambda i, h, t: (i, t, z0 + h)),
                  pl.BlockSpec((1, tc, HEAD_DIM), lambda i, h, t: (i, t, 0)),
                  pl.BlockSpec((GDN_CONV, qw), lambda i, h, t: (0, h)),
                  pl.BlockSpec((GDN_CONV, qw), lambda i, h, t: (0, steps + h)),
                  pl.BlockSpec((GDN_CONV, vw), lambda i, h, t: (0, v0 + h)),
                  pl.BlockSpec((1, HEAD_DIM), lambda i, h, t: (0, 0))],
        out_specs=pl.BlockSpec((1, tc, vw), lambda i, h, t: (i, t, h)),
        out_shape=jax.ShapeDtypeStruct((b, s, v_heads * HEAD_DIM), BF16),
        scratch_shapes=[pltpu.VMEM((nh, HEAD_DIM, HEAD_DIM), F32),
                        pltpu.VMEM((tc + SUBLANES, qw), F32),
                        pltpu.VMEM((tc + SUBLANES, qw), F32),
                        pltpu.VMEM((tc + SUBLANES, vw), F32),
                        pltpu.VMEM((nh, tc, HEAD_DIM), F32),
                        pltpu.VMEM((nh, n_chunks, HEAD_DIM, HEAD_DIM), BF16),
                        pltpu.VMEM((nh, n_chunks, HEAD_DIM, HEAD_DIM), F32),
                        pltpu.VMEM((nh, tc, HEAD_DIM), BF16),
                        pltpu.VMEM((nh, tc, HEAD_DIM), F32)],
        compiler_params=_cparams(("parallel", "parallel", "arbitrary")),
        name="gdn_recurrence",
    )(proj, proj, proj, proj, gates, conv_w, conv_w, conv_w, norm_w.astype(F32).reshape(1, HEAD_DIM))


def kernel(x, c, ada_w, ada_b, norm_w, hg_w_in, hg_lb_logits, hg_norm_w, hg_w_out, gdn_w_in,
           gdn_conv_w, gdn_A_log, gdn_dt_bias, gdn_norm_w, gdn_w_out, ffn_w_gate_up, ffn_w_down):
    depth = ada_w.shape[0]
    b, s, d = x.shape
    v_heads = gdn_A_log.shape[1]
    k_heads = v_heads // 2
    gdn_main = 2 * k_heads * HEAD_DIM + 2 * v_heads * HEAD_DIM

    mod_all = ada_modulation(c, ada_w, ada_b).reshape(depth, b, N_MOD, d)
    for layer in range(depth):
        mod = mod_all[layer]
        nw = norm_w[layer]
        j = layer // 2
        if layer % 2 == 0:
            proj = norm_mod_project(x, nw[0:1], mod, hg_w_in[j].astype(BF16), 0, F32)
            mixed = hgrn2_recurrence(proj, hg_lb_logits, hg_norm_w[j], j)
            w_out = hg_w_out[j]
        else:
            w_in = gdn_w_in[j]
            w_tail = jnp.pad(w_in[:, gdn_main:], ((0, 0), (0, HEAD_DIM - 2 * v_heads)))
            proj, ba = norm_mod_project(x, nw[0:1], mod, w_in.astype(BF16), 0, BF16, n=gdn_main,
                                        w_tail=w_tail.astype(BF16))
            gates = gdn_gates(ba, gdn_A_log[j], gdn_dt_bias[j], v_heads)
            mixed = gdn_recurrence(proj, gates, gdn_conv_w[j], gdn_norm_w[j], k_heads, v_heads)
            w_out = gdn_w_out[j]
        x = out_project_residual(mixed, w_out.astype(BF16), x, nw[1:2], mod, 2)
        x = ffn_residual(x, nw[2:3], nw[3:4], mod, ffn_w_gate_up[layer].astype(BF16),
                         ffn_w_down[layer].astype(BF16))
    return x
```

```python
import functools

import jax
import jax.numpy as jnp
from jax import lax
from jax.experimental import pallas as pl
from jax.experimental.pallas import tpu as pltpu

F32 = jnp.float32
BF16 = jnp.bfloat16
EPS = 1e-6
N_MOD = 6
HEAD_DIM = 128
GDN_CONV = 4
SUBLANES = 8
GDN_CHUNK = 64
HG_CHUNK = 128
HG_LEAF = 8
NEG = -1e30
LOG2E = 1.4426950408889634
ROWS_PER_PASS = 256
VMEM_LIMIT = 56 * 1024 * 1024


def _cparams(sem):
    return pltpu.CompilerParams(dimension_semantics=sem, vmem_limit_bytes=VMEM_LIMIT)


def _tile(n, pref):
    t = min(n, pref)
    while n % t:
        t //= 2
    return t


def _sigmoid(x):
    return 0.5 * jnp.tanh(0.5 * x) + 0.5


def _silu(x):
    h = 0.5 * x
    return h * jnp.tanh(h) + h


def _bdot(a, b):
    return jnp.dot(a.astype(BF16), b.astype(BF16), preferred_element_type=F32)


def _bdot_nt(a, b):
    return lax.dot_general(a.astype(BF16), b.astype(BF16), (((1,), (1,)), ((), ())),
                           preferred_element_type=F32)


def _bdot_tn(a, b):
    return lax.dot_general(a.astype(BF16), b.astype(BF16), (((0,), (0,)), ((), ())),
                           preferred_element_type=F32)


def _split_dot(a_exact_bf16, x):
    hi = x.astype(BF16)
    lo = (x - hi.astype(F32)).astype(BF16)
    return (jnp.dot(a_exact_bf16, hi, preferred_element_type=F32)
            + jnp.dot(a_exact_bf16, lo, preferred_element_type=F32))


def _rms(x, w):
    ms = jnp.mean(x * x, axis=-1, keepdims=True)
    return x * lax.rsqrt(ms + EPS) * w


def _ada_kernel(c_ref, w_ref, b_ref, o_ref):
    c = c_ref[...]
    o_ref[0] = _bdot(_silu(c), w_ref[0]) + b_ref[0]


def ada_modulation(c, ada_w, ada_b):
    depth, d, n = ada_w.shape
    b = c.shape[0]
    tn = _tile(n, 2048)
    return pl.pallas_call(
        _ada_kernel,
        grid=(depth, n // tn),
        in_specs=[pl.BlockSpec((b, d), lambda l, j: (0, 0)),
                  pl.BlockSpec((1, d, tn), lambda l, j: (l, 0, j)),
                  pl.BlockSpec((1, 1, tn), lambda l, j: (l, 0, j))],
        out_specs=pl.BlockSpec((1, b, tn), lambda l, j: (l, 0, j)),
        out_shape=jax.ShapeDtypeStruct((depth, b, n), F32),
        compiler_params=_cparams(("parallel", "parallel")),
        name="ada_modulation",
    )(c, ada_w, ada_b.reshape(depth, 1, n))


def _row_passes(rows, per=ROWS_PER_PASS):
    per = _tile(rows, per)
    return [slice(r * per, (r + 1) * per) for r in range(rows // per)]


def _inv_rms(load):
    x = load()
    return lax.rsqrt(jnp.mean(x * x, axis=-1, keepdims=True) + EPS)


def _norm_mod_rows(x_ref, rows, nw, mod_ref, shift_row):
    gain = nw * (1.0 + mod_ref[0, shift_row + 1:shift_row + 2, :])
    shift = mod_ref[0, shift_row:shift_row + 1, :]
    load = lambda: x_ref[0, rows, :]
    return load() * _inv_rms(load) * gain + shift


def _proj_kernel(x_ref, nw_ref, mod_ref, w_ref, *rest, shift_row, with_tail):
    if with_tail:
        wt_ref, o_ref, ot_ref, h_ref = rest
    else:
        o_ref, h_ref = rest
    tm = h_ref.shape[0]

    def body(first):
        for rows in _row_passes(tm, 512):
            if first:
                for sub in _row_passes(rows.stop - rows.start):
                    rr = slice(rows.start + sub.start, rows.start + sub.stop)
                    h_ref[rr, :] = _norm_mod_rows(x_ref, rr, nw_ref[...], mod_ref, shift_row).astype(BF16)
            o_ref[0, rows, :] = jnp.dot(h_ref[rows, :], w_ref[...],
                                        preferred_element_type=F32).astype(o_ref.dtype)
            if first and with_tail:
                ot_ref[0, rows, :] = jnp.dot(h_ref[rows, :], wt_ref[...], preferred_element_type=F32)

    @pl.when(pl.program_id(2) == 0)
    def _():
        body(True)

    @pl.when(pl.program_id(2) > 0)
    def _():
        body(False)


def norm_mod_project(x, nw, mod, w, shift_row, out_dtype, n=None, w_tail=None):
    b, s, d = x.shape
    n = w.shape[1] if n is None else n
    tm = _tile(s, 1024)
    tn = _tile(n, 2048)
    in_specs = [pl.BlockSpec((1, tm, d), lambda i, m, j: (i, m, 0)),
                pl.BlockSpec((1, d), lambda i, m, j: (0, 0)),
                pl.BlockSpec((1, N_MOD, d), lambda i, m, j: (i, 0, 0)),
                pl.BlockSpec((d, tn), lambda i, m, j: (0, j))]
    out_specs = pl.BlockSpec((1, tm, tn), lambda i, m, j: (i, m, j))
    out_shape = jax.ShapeDtypeStruct((b, s, n), out_dtype)
    args = (x, nw, mod, w)
    if w_tail is not None:
        nt = w_tail.shape[1]
        in_specs.append(pl.BlockSpec((d, nt), lambda i, m, j: (0, 0)))
        out_specs = [out_specs, pl.BlockSpec((1, tm, nt), lambda i, m, j: (i, m, 0))]
        out_shape = [out_shape, jax.ShapeDtypeStruct((b, s, nt), F32)]
        args = args + (w_tail,)
    return pl.pallas_call(
        functools.partial(_proj_kernel, shift_row=shift_row, with_tail=w_tail is not None),
        grid=(b, s // tm, n // tn),
        in_specs=in_specs,
        out_specs=out_specs,
        out_shape=out_shape,
        scratch_shapes=[pltpu.VMEM((tm, d), BF16)],
        compiler_params=_cparams(("parallel", "parallel", "arbitrary")),
        name="norm_mod_project",
    )(*args)


def _out_kernel(a_ref, w_ref, x_ref, nw_ref, mod_ref, o_ref, *, gate_row):
    o_ref[0] = jnp.dot(a_ref[0], w_ref[...], preferred_element_type=F32)
    gain = mod_ref[0, gate_row:gate_row + 1, :] * nw_ref[...]
    for rows in _row_passes(o_ref.shape[1]):
        load = lambda: o_ref[0, rows, :]
        o_ref[0, rows, :] = x_ref[0, rows, :] + load() * _inv_rms(load) * gain


def out_project_residual(a, w, x, nw, mod, gate_row):
    b, s, k = a.shape
    d = w.shape[1]
    tm = _tile(s, 512)
    return pl.pallas_call(
        functools.partial(_out_kernel, gate_row=gate_row),
        grid=(b, s // tm),
        in_specs=[pl.BlockSpec((1, tm, k), lambda i, m: (i, m, 0)),
                  pl.BlockSpec((k, d), lambda i, m: (0, 0), pipeline_mode=pl.Buffered(1)),
                  pl.BlockSpec((1, tm, d), lambda i, m: (i, m, 0)),
                  pl.BlockSpec((1, d), lambda i, m: (0, 0)),
                  pl.BlockSpec((1, N_MOD, d), lambda i, m: (i, 0, 0))],
        out_specs=pl.BlockSpec((1, tm, d), lambda i, m: (i, m, 0)),
        out_shape=jax.ShapeDtypeStruct((b, s, d), F32),
        compiler_params=_cparams(("parallel", "parallel")),
        name="out_project_residual",
    )(a, w, x, nw, mod)


def _ffn_kernel(x_ref, nw_in_ref, mod_ref, wg_ref, wu_ref, wd_ref, nw_out_ref, o_ref, h_ref):
    k = pl.program_id(2)
    nk = pl.num_programs(2)
    tm = h_ref.shape[0]

    def body(first, last):
        for rows in _row_passes(tm, 512):
            if first:
                for sub in _row_passes(rows.stop - rows.start):
                    rr = slice(rows.start + sub.start, rows.start + sub.stop)
                    h_ref[rr, :] = _norm_mod_rows(x_ref, rr, nw_in_ref[...], mod_ref, 3).astype(BF16)
            h = h_ref[rows, :]
            g = jnp.dot(h, wg_ref[...], preferred_element_type=F32)
            u = jnp.dot(h, wu_ref[...], preferred_element_type=F32)
            part = jnp.dot((_silu(g) * u).astype(BF16), wd_ref[...], preferred_element_type=F32)
            if first:
                o_ref[0, rows, :] = part
            else:
                o_ref[0, rows, :] += part
            if last:
                gain = mod_ref[0, 5:6, :] * nw_out_ref[...]
                for sub in _row_passes(rows.stop - rows.start):
                    rr = slice(rows.start + sub.start, rows.start + sub.stop)
                    load = lambda: o_ref[0, rr, :]
                    o_ref[0, rr, :] = x_ref[0, rr, :] + load() * _inv_rms(load) * gain

    @pl.when((k == 0) & (nk > 1))
    def _():
        body(True, False)

    @pl.when((k > 0) & (k < nk - 1))
    def _():
        body(False, False)

    @pl.when((k == nk - 1) & (nk > 1))
    def _():
        body(False, True)

    @pl.when(nk == 1)
    def _():
        body(True, True)


def ffn_residual(x, nw_in, nw_out, mod, w_gate_up, w_down):
    b, s, d = x.shape
    f = w_down.shape[0]
    tm = _tile(s, 1024)
    tf = _tile(f, 512)
    nf = f // tf
    return pl.pallas_call(
        _ffn_kernel,
        grid=(b, s // tm, nf),
        in_specs=[pl.BlockSpec((1, tm, d), lambda i, m, k: (i, m, 0)),
                  pl.BlockSpec((1, d), lambda i, m, k: (0, 0)),
                  pl.BlockSpec((1, N_MOD, d), lambda i, m, k: (i, 0, 0)),
                  pl.BlockSpec((d, tf), lambda i, m, k: (0, k)),
                  pl.BlockSpec((d, tf), lambda i, m, k: (0, k + nf)),
                  pl.BlockSpec((tf, d), lambda i, m, k: (k, 0)),
                  pl.BlockSpec((1, d), lambda i, m, k: (0, 0))],
        out_specs=pl.BlockSpec((1, tm, d), lambda i, m, k: (i, m, 0)),
        out_shape=jax.ShapeDtypeStruct((b, s, d), F32),
        scratch_shapes=[pltpu.VMEM((tm, d), BF16)],
        compiler_params=_cparams(("parallel", "parallel", "arbitrary")),
        name="ffn_residual",
    )(x, nw_in, mod, w_gate_up, w_gate_up, w_down, nw_out)


def _row_group_bcast(x, group, row):
    c, w = x.shape
    x3 = x.reshape(c // group, group, w)
    return jnp.broadcast_to(x3[:, row:row + 1, :], x3.shape).reshape(c, w)


def _hgrn_kernel(q_ref, f_ref, v_ref, g_ref, lbl_ref, nw_ref, o_ref, state_ref, kleaf_ref, bleaf_ref,
                 *, layer_j, hg):
    c = HG_CHUNK
    hd = HEAD_DIM
    n_chunks = q_ref.shape[1] // c
    pairs = [(j, m) for j in range(n_chunks) for m in range(hg)]

    @pl.when(pl.program_id(2) == 0)
    def _():
        state_ref[...] = jnp.zeros_like(state_ref)

    log_lb, log_1mlb = [], []
    for m in range(hg):
        logits = lbl_ref[m]
        e = jnp.exp(logits - jnp.max(logits, axis=0, keepdims=True))
        sm = e / jnp.sum(e, axis=0, keepdims=True)
        lb = jnp.sum(sm[:layer_j + 1], axis=0, keepdims=True) - sm[0:1]
        log_lb.append(jnp.log(lb))
        log_1mlb.append(jnp.log1p(-lb))

    row = lax.broadcasted_iota(jnp.int32, (c, c), 0)
    col = lax.broadcasted_iota(jnp.int32, (c, c), 1)
    rowk = lax.broadcasted_iota(jnp.int32, (c, hd), 0)
    tril = (row >= col).astype(BF16)

    def blk(ref, j, m):
        return ref[0, j * c:(j + 1) * c, m * hd:(m + 1) * hd]

    q = {(j, m): _silu(blk(q_ref, j, m)) for j, m in pairs}
    v = {(j, m): blk(v_ref, j, m) for j, m in pairs}
    log_f = {}
    for j, m in pairs:
        f_raw = blk(f_ref, j, m)
        log_sig = jnp.minimum(f_raw, 0.0) - jnp.log(1.0 + jnp.exp(-jnp.abs(f_raw)))
        cand = log_1mlb[m] + log_sig
        log_f[j, m] = jnp.maximum(log_lb[m], cand) + jnp.log(1.0 + jnp.exp(-jnp.abs(log_lb[m] - cand)))
    k = {jm: 1.0 - jnp.exp(log_f[jm]) for jm in pairs}
    b = {jm: _split_dot(tril, log_f[jm]) * LOG2E for jm in pairs}
    b_last = {jm: b[jm][c - 1:c, :] for jm in pairs}

    scores = {jm: jnp.zeros((c, c), F32) for jm in pairs}
    n = c
    while n > HG_LEAF:
        half = n // 2
        right = (rowk % n) >= half
        same = (row // n) == (col // n)
        qt, kt = {}, {}
        for jm in pairs:
            r = _row_group_bcast(b[jm], n, half - 1)
            qt[jm] = q[jm] * jnp.exp2(jnp.where(right, b[jm] - r, NEG))
            kt[jm] = k[jm] * jnp.exp2(jnp.where(right, NEG, r - b[jm]))
        scores = {jm: scores[jm] + jnp.where(same, _bdot_nt(qt[jm], kt[jm]), 0.0) for jm in pairs}
        n = half

    g = c // HG_LEAF
    t_in = lax.broadcasted_iota(jnp.int32, (g, HG_LEAF, hd), 1)
    col3 = lax.broadcasted_iota(jnp.int32, (g, HG_LEAF, c), 2)
    blk0 = lax.broadcasted_iota(jnp.int32, (g, HG_LEAF, c), 0) * HG_LEAF
    for p_idx, jm in enumerate(pairs):
        q3 = q[jm].reshape(g, HG_LEAF, hd)
        b3 = b[jm].reshape(g, HG_LEAF, hd)
        s3 = scores[jm].reshape(g, HG_LEAF, c)
        kleaf_ref[p_idx] = k[jm].reshape(g, HG_LEAF, hd)
        bleaf_ref[p_idx] = b3
        for s in range(HG_LEAF):
            ks = kleaf_ref[p_idx, :, pl.ds(s, 1), :]
            bs = bleaf_ref[p_idx, :, pl.ds(s, 1), :]
            p = q3 * ks * jnp.exp2(jnp.where(t_in >= s, b3 - bs, NEG))
            s3 = jnp.where(col3 == blk0 + s, jnp.sum(p, axis=-1, keepdims=True), s3)
        scores[jm] = s3.reshape(c, c)
    o = {jm: _bdot(scores[jm], v[jm]) for jm in pairs}

    qe = {jm: (q[jm] * jnp.exp2(b[jm])).astype(BF16) for jm in pairs}
    upd = {jm: _bdot_tn(v[jm], k[jm] * jnp.exp2(b_last[jm] - b[jm])) for jm in pairs}

    state = [state_ref[m] for m in range(hg)]
    for j in range(n_chunks):
        for m in range(hg):
            o[j, m] = o[j, m] + _bdot_nt(qe[j, m], state[m])
            state[m] = state[m] * jnp.exp2(b_last[j, m]) + upd[j, m]
    for m in range(hg):
        state_ref[m] = state[m]

    nw = nw_ref[...]
    for j, m in pairs:
        gate = _sigmoid(blk(g_ref, j, m))
        o_ref[0, j * c:(j + 1) * c, m * hd:(m + 1) * hd] = (_rms(o[j, m], nw) * gate).astype(o_ref.dtype)


def hgrn2_recurrence(proj, lb_logits, norm_w, layer_j):
    b, s, n4 = proj.shape
    hd = n4 // 4
    heads = hd // HEAD_DIM
    hg = 4 if heads % 4 == 0 else 1
    steps = heads // hg
    tc = _tile(s, 512)
    nl = lb_logits.shape[0]
    lbl = lb_logits.astype(F32).reshape(nl, heads, HEAD_DIM).transpose(1, 0, 2)
    leaf_shape = (hg * (tc // HG_CHUNK), HG_CHUNK // HG_LEAF, HG_LEAF, HEAD_DIM)

    def col(section):
        return pl.BlockSpec((1, tc, hg * HEAD_DIM), lambda i, h, t: (i, t, section * steps + h))

    return pl.pallas_call(
        functools.partial(_hgrn_kernel, layer_j=layer_j, hg=hg),
        grid=(b, steps, s // tc),
        in_specs=[col(0), col(1), col(2), col(3),
                  pl.BlockSpec((hg, nl, HEAD_DIM), lambda i, h, t: (h, 0, 0)),
                  pl.BlockSpec((1, HEAD_DIM), lambda i, h, t: (0, 0))],
        out_specs=pl.BlockSpec((1, tc, hg * HEAD_DIM), lambda i, h, t: (i, t, h)),
        out_shape=jax.ShapeDtypeStruct((b, s, hd), BF16),
        scratch_shapes=[pltpu.VMEM((hg, HEAD_DIM, HEAD_DIM), F32),
                        pltpu.VMEM(leaf_shape, F32),
                        pltpu.VMEM(leaf_shape, F32)],
        compiler_params=_cparams(("parallel", "parallel", "arbitrary")),
        name="hgrn2_recurrence",
    )(proj, proj, proj, proj, lbl, norm_w.astype(F32).reshape(1, HEAD_DIM))


def _l2n(x):
    return x * lax.rsqrt(jnp.sum(x * x, axis=-1, keepdims=True) + EPS)


def _gdn_gates_kernel(ba_ref, ga_ref, o_ref, *, v_heads):
    ba = ba_ref[0]
    t = ba.shape[0]
    beta = _sigmoid(ba)
    xa = ba + ga_ref[1:2, :]
    g = -jnp.exp(ga_ref[0:1, :]) * (jnp.maximum(xa, 0.0) + jnp.log1p(jnp.exp(-jnp.abs(xa))))
    row = lax.broadcasted_iota(jnp.int32, (t, t), 0)
    col = lax.broadcasted_iota(jnp.int32, (t, t), 1)
    block_tril = jnp.where((row >= col) & (row // GDN_CHUNK == col // GDN_CHUNK), 1.0, 0.0).astype(BF16)
    gc = _split_dot(block_tril, g)
    lane = lax.broadcasted_iota(jnp.int32, ba.shape, 1)
    o_ref[0] = jnp.where(lane < v_heads, beta, gc)


def gdn_gates(ba, a_log, dt_bias, v_heads):
    b, s, w = ba.shape
    tc = _tile(s, 512)
    pad = w - 2 * v_heads
    zeros = jnp.zeros((v_heads,), F32)
    gate_params = jnp.stack([
        jnp.concatenate([zeros, a_log.astype(F32), jnp.zeros((pad,), F32)]),
        jnp.concatenate([zeros, dt_bias.astype(F32), jnp.zeros((pad,), F32)]),
    ])
    return pl.pallas_call(
        functools.partial(_gdn_gates_kernel, v_heads=v_heads),
        grid=(b, s // tc),
        in_specs=[pl.BlockSpec((1, tc, w), lambda i, t: (i, t, 0)),
                  pl.BlockSpec((2, w), lambda i, t: (0, 0))],
        out_specs=pl.BlockSpec((1, tc, w), lambda i, t: (i, t, 0)),
        out_shape=jax.ShapeDtypeStruct((b, s, w), F32),
        compiler_params=_cparams(("parallel", "parallel")),
        name="gdn_gates",
    )(ba, gate_params)


def _conv_silu(xp_ref, w, lanes, nrows):
    out = xp_ref[SUBLANES:SUBLANES + nrows, lanes] * w[GDN_CONV - 1:GDN_CONV, :]
    for back in range(1, GDN_CONV):
        out = out + xp_ref[pl.ds(SUBLANES - back, nrows), lanes] * w[GDN_CONV - 1 - back:GDN_CONV - back, :]
    return _silu(out)


def _gdn_kernel(q_ref, k_ref, v_ref, z_ref, gates_ref, cq_ref, ck_ref, cv_ref, nw_ref,
                o_ref, state_ref, xq_ref, xk_ref, xv_ref, gc_ref,
                w2_ref, n_ref, qp_ref, o0_ref, *, kg, rep, v_heads):
    tc = q_ref.shape[1]
    hd = HEAD_DIM
    c = GDN_CHUNK
    n_chunks = tc // c
    nh = kg * rep

    @pl.when(pl.program_id(2) == 0)
    def _():
        state_ref[...] = jnp.zeros_like(state_ref)
        xq_ref[tc:, :] = jnp.zeros((SUBLANES, xq_ref.shape[1]), F32)
        xk_ref[tc:, :] = jnp.zeros((SUBLANES, xk_ref.shape[1]), F32)
        xv_ref[tc:, :] = jnp.zeros((SUBLANES, xv_ref.shape[1]), F32)

    row = lax.broadcasted_iota(jnp.int32, (c, c), 0)
    col = lax.broadcasted_iota(jnp.int32, (c, c), 1)
    eye = (row == col).astype(F32)
    rows = [slice(j * c, (j + 1) * c) for j in range(n_chunks)]

    for xp_ref, raw_ref in ((xq_ref, q_ref), (xk_ref, k_ref), (xv_ref, v_ref)):
        xp_ref[0:SUBLANES, :] = xp_ref[tc:tc + SUBLANES, :]
        xp_ref[SUBLANES:, :] = raw_ref[0].astype(F32)

    def prepare(khs):
        q, k, v, beta, gcb = {}, {}, {}, {}, {}
        gates = gates_ref[0]
        lane = lax.broadcasted_iota(jnp.int32, gates.shape, 1)
        for g in khs:
            lanes = slice(g * hd, (g + 1) * hd)
            qn = _l2n(_conv_silu(xq_ref, cq_ref[:, lanes], lanes, tc)) * (hd ** -0.5)
            kn = _l2n(_conv_silu(xk_ref, ck_ref[:, lanes], lanes, tc))
            for j in range(n_chunks):
                q[j, g] = qn[rows[j], :]
                k[j, g] = kn[rows[j], :]
            for m in range(g * rep, (g + 1) * rep):
                vlanes = slice(m * hd, (m + 1) * hd)
                va = _conv_silu(xv_ref, cv_ref[:, vlanes], vlanes, tc)
                vh = pl.program_id(1) * nh + m
                beta_col = jnp.sum(jnp.where(lane == vh, gates, 0.0), axis=-1, keepdims=True)
                gc_col = jnp.sum(jnp.where(lane == vh + v_heads, gates, 0.0), axis=-1, keepdims=True)
                beta_b = jnp.broadcast_to(beta_col, (tc, hd))
                gc_b = jnp.broadcast_to(gc_col, (tc, hd))
                gc_ref[m] = gc_b
                for j in range(n_chunks):
                    v[j, m] = va[rows[j], :]
                    beta[j, m] = beta_b[rows[j], :]
                    gcb[j, m] = gc_b[rows[j], :]
        return q, k, v, beta, gcb

    def solve(khs, prepared):
        q, k, v, beta, gcb = prepared
        pairs = [(j, m) for j in range(n_chunks) for g in khs for m in range(g * rep, (g + 1) * rep)]
        kpairs = [(j, g) for j in range(n_chunks) for g in khs]
        kb = {jg: k[jg].astype(BF16) for jg in kpairs}
        kk = {jg: _bdot_nt(kb[jg], kb[jg]) for jg in kpairs}
        qk = {jg: _bdot_nt(q[jg], kb[jg]) for jg in kpairs}
        decay = {jm: jnp.exp(jnp.where(row >= col, gcb[jm][:, 0:c] - jnp.transpose(gcb[jm])[0:1, :], NEG))
                 for jm in pairs}
        a = {(j, m): beta[j, m][:, 0:c] * kk[j, m // rep] * decay[j, m] for j, m in pairs}
        t_inv = {jm: eye - jnp.where((row // 2 == col // 2) & (row > col), a[jm], 0.0) for jm in pairs}
        ab = {jm: a[jm].astype(BF16) for jm in pairs}
        tb = {jm: t_inv[jm].astype(BF16) for jm in pairs}
        zero = jnp.zeros((c, c), BF16)
        node = 4
        while node <= c:
            links = (row // node == col // node) & (row % node >= node // 2) & (col % node < node // 2)
            et = {jm: jnp.dot(jnp.where(links, ab[jm], zero), tb[jm], preferred_element_type=F32).astype(BF16)
                  for jm in pairs}
            t_inv = {jm: t_inv[jm] - jnp.dot(tb[jm], et[jm], preferred_element_type=F32) for jm in pairs}
            tb = {jm: t_inv[jm].astype(BF16) for jm in pairs}
            node *= 2
        e_gc = {jm: jnp.exp(gcb[jm]) for jm in pairs}
        ub = {jm: _bdot(tb[jm], v[jm] * beta[jm]).astype(BF16) for jm in pairs}
        wb = {(j, m): _bdot(tb[j, m], k[j, m // rep] * (beta[j, m] * e_gc[j, m])).astype(BF16)
              for j, m in pairs}
        attn = {(j, m): (qk[j, m // rep] * decay[j, m]).astype(BF16) for j, m in pairs}
        kdt = {(j, m): jnp.transpose(k[j, m // rep] * jnp.exp(gcb[j, m][c - 1:c, :] - gcb[j, m])).astype(BF16)
               for j, m in pairs}
        for j, m in pairs:
            w2_ref[m, j] = jnp.dot(kdt[j, m], wb[j, m], preferred_element_type=F32).astype(BF16)
        for j, m in pairs:
            n_ref[m, j] = jnp.dot(kdt[j, m], ub[j, m], preferred_element_type=F32)
        for j, m in pairs:
            qp_ref[m, rows[j], :] = (q[j, m // rep] * e_gc[j, m]
                                     - jnp.dot(attn[j, m], wb[j, m], preferred_element_type=F32)).astype(BF16)
        for j, m in pairs:
            o0_ref[m, rows[j], :] = jnp.dot(attn[j, m], ub[j, m], preferred_element_type=F32)

    half = max(kg // 2, 1)
    groups = [range(g0, min(g0 + half, kg)) for g0 in range(0, kg, half)]
    prepared = [prepare(khs) for khs in groups]
    for khs, prep in zip(groups, prepared):
        solve(khs, prep)

    heads = range(nh)
    nw = nw_ref[...]

    def advance(i):
        sl = pl.ds(pl.multiple_of(i * c, c), c)
        state = [state_ref[m] for m in heads]
        sb = [x.astype(BF16) for x in state]
        ws = [jnp.dot(w2_ref[m, i], sb[m], preferred_element_type=F32) for m in heads]
        o = [jnp.dot(qp_ref[m, sl, :], sb[m], preferred_element_type=F32) for m in heads]
        for m in heads:
            g_last = gc_ref[m, pl.ds(i * c + c - 1, 1), :]
            state_ref[m] = state[m] * jnp.exp(g_last) - ws[m] + n_ref[m, i]
        for m in heads:
            o0_ref[m, sl, :] += o[m]

    def finish(i):
        sl = pl.ds(pl.multiple_of(i * c, c), c)
        for m in heads:
            lanes = slice(m * hd, (m + 1) * hd)
            z = z_ref[0, sl, lanes].astype(F32)
            o_ref[0, sl, lanes] = (_rms(o0_ref[m, sl, :], nw) * _silu(z)).astype(o_ref.dtype)

    advance(0)

    def body(i, carry):
        finish(i - 1)
        advance(i)
        return carry

    lax.fori_loop(1, n_chunks, body, 0, unroll=True)
    finish(n_chunks - 1)


def gdn_recurrence(proj, gates, conv_w, norm_w, k_heads, v_heads):
    b, s, _ = proj.shape
    rep = v_heads // k_heads
    kg = 4 if k_heads % 4 == 0 else 1
    nh = kg * rep
    qw = kg * HEAD_DIM
    vw = nh * HEAD_DIM
    tc = _tile(s, 512)
    c = GDN_CHUNK
    n_chunks = tc // c
    steps = k_heads // kg
    v0 = 2 * k_heads * HEAD_DIM // vw
    z0 = v0 + v_heads * HEAD_DIM // vw
    conv_w = conv_w.astype(F32)

    return pl.pallas_call(
        functools.partial(_gdn_kernel, kg=kg, rep=rep, v_heads=v_heads),
        grid=(b, steps, s // tc),
        in_specs=[pl.BlockSpec((1, tc, qw), lambda i, h, t: (i, t, h)),
                  pl.BlockSpec((1, tc, qw), lambda i, h, t: (i, t, steps + h)),
                  pl.BlockSpec((1, tc, vw), lambda i, h, t: (i, t, v0 + h)),
                  pl.BlockSpec((1, tc, vw), lambda i, h, t: (i, t, z0 + h)),
                  pl.BlockSpec((1, tc, HEAD_DIM), lambda i, h, t: (i, t, 0)),
                  pl.BlockSpec((GDN_CONV, qw), lambda i, h, t: (0, h)),
                  pl.BlockSpec((GDN_CONV, qw), lambda i, h, t: (0, steps + h)),
                  pl.BlockSpec((GDN_CONV, vw), lambda i, h, t: (0, v0 + h)),
                  pl.BlockSpec((1, HEAD_DIM), lambda i, h, t: (0, 0))],
        out_specs=pl.BlockSpec((1, tc, vw), lambda i, h, t: (i, t, h)),
        out_shape=jax.ShapeDtypeStruct((b, s, v_heads * HEAD_DIM), BF16),
        scratch_shapes=[pltpu.VMEM((nh, HEAD_DIM, HEAD_DIM), F32),
                        pltpu.VMEM((tc + SUBLANES, qw), F32),
                        pltpu.VMEM((tc + SUBLANES, qw), F32),
                        pltpu.VMEM((tc + SUBLANES, vw), F32),
                        pltpu.VMEM((nh, tc, HEAD_DIM), F32),
                        pltpu.VMEM((nh, n_chunks, HEAD_DIM, HEAD_DIM), BF16),
                        pltpu.VMEM((nh, n_chunks, HEAD_DIM, HEAD_DIM), F32),
                        pltpu.VMEM((nh, tc, HEAD_DIM), BF16),
                        pltpu.VMEM((nh, tc, HEAD_DIM), F32)],
        compiler_params=_cparams(("parallel", "parallel", "arbitrary")),
        name="gdn_recurrence",
    )(proj, proj, proj, proj, gates, conv_w, conv_w, conv_w, norm_w.astype(F32).reshape(1, HEAD_DIM))


def kernel(x, c, ada_w, ada_b, norm_w, hg_w_in, hg_lb_logits, hg_norm_w, hg_w_out, gdn_w_in,
           gdn_conv_w, gdn_A_log, gdn_dt_bias, gdn_norm_w, gdn_w_out, ffn_w_gate_up, ffn_w_down):
    depth = ada_w.shape[0]
    b, s, d = x.shape
    v_heads = gdn_A_log.shape[1]
    k_heads = v_heads // 2
    gdn_main = 2 * k_heads * HEAD_DIM + 2 * v_heads * HEAD_DIM

    mod_all = ada_modulation(c, ada_w, ada_b).reshape(depth, b, N_MOD, d)
    for layer in range(depth):
        mod = mod_all[layer]
        nw = norm_w[layer]
        j = layer // 2
        if layer % 2 == 0:
            proj = norm_mod_project(x, nw[0:1], mod, hg_w_in[j].astype(BF16), 0, F32)
            mixed = hgrn2_recurrence(proj, hg_lb_logits, hg_norm_w[j], j)
            w_out = hg_w_out[j]
        else:
            w_in = gdn_w_in[j]
            w_tail = jnp.pad(w_in[:, gdn_main:], ((0, 0), (0, HEAD_DIM - 2 * v_heads)))
            proj, ba = norm_mod_project(x, nw[0:1], mod, w_in.astype(BF16), 0, BF16, n=gdn_main,
                                        w_tail=w_tail.astype(BF16))
            gates = gdn_gates(ba, gdn_A_log[j], gdn_dt_bias[j], v_heads)
            mixed = gdn_recurrence(proj, gates, gdn_conv_w[j], gdn_norm_w[j], k_heads, v_heads)
            w_out = gdn_w_out[j]
        x = out_project_residual(mixed, w_out.astype(BF16), x, nw[1:2], mod, 2)
        x = ffn_residual(x, nw[2:3], nw[3:4], mod, ffn_w_gate_up[layer].astype(BF16),
                         ffn_w_down[layer].astype(BF16))
    return x
```

```python
import functools

import jax
import jax.numpy as jnp
from jax import lax
from jax.experimental import pallas as pl
from jax.experimental.pallas import tpu as pltpu

F32 = jnp.float32
BF16 = jnp.bfloat16
EPS = 1e-6
N_MOD = 6
HEAD_DIM = 128
GDN_CONV = 4
SUBLANES = 8
GDN_CHUNK = 64
HG_CHUNK = 128
HG_LEAF = 8
NEG = -1e30
LOG2E = 1.4426950408889634
ROWS_PER_PASS = 256
VMEM_LIMIT = 56 * 1024 * 1024


def _cparams(sem):
    return pltpu.CompilerParams(dimension_semantics=sem, vmem_limit_bytes=VMEM_LIMIT)


def _tile(n, pref):
    t = min(n, pref)
    while n % t:
        t //= 2
    return t


def _sigmoid(x):
    return 0.5 * jnp.tanh(0.5 * x) + 0.5


def _silu(x):
    h = 0.5 * x
    return h * jnp.tanh(h) + h


def _bdot(a, b):
    return jnp.dot(a.astype(BF16), b.astype(BF16), preferred_element_type=F32)


def _bdot_nt(a, b):
    return lax.dot_general(a.astype(BF16), b.astype(BF16), (((1,), (1,)), ((), ())),
                           preferred_element_type=F32)


def _bdot_tn(a, b):
    return lax.dot_general(a.astype(BF16), b.astype(BF16), (((0,), (0,)), ((), ())),
                           preferred_element_type=F32)


def _split_dot(a_exact_bf16, x):
    hi = x.astype(BF16)
    lo = (x - hi.astype(F32)).astype(BF16)
    return (jnp.dot(a_exact_bf16, hi, preferred_element_type=F32)
            + jnp.dot(a_exact_bf16, lo, preferred_element_type=F32))


def _rms(x, w):
    ms = jnp.mean(x * x, axis=-1, keepdims=True)
    return x * lax.rsqrt(ms + EPS) * w


def _ada_kernel(c_ref, w_ref, b_ref, o_ref):
    c = c_ref[...]
    o_ref[0] = _bdot(_silu(c), w_ref[0]) + b_ref[0]


def ada_modulation(c, ada_w, ada_b):
    depth, d, n = ada_w.shape
    b = c.shape[0]
    tn = _tile(n, 2048)
    return pl.pallas_call(
        _ada_kernel,
        grid=(depth, n // tn),
        in_specs=[pl.BlockSpec((b, d), lambda l, j: (0, 0)),
                  pl.BlockSpec((1, d, tn), lambda l, j: (l, 0, j)),
                  pl.BlockSpec((1, 1, tn), lambda l, j: (l, 0, j))],
        out_specs=pl.BlockSpec((1, b, tn), lambda l, j: (l, 0, j)),
        out_shape=jax.ShapeDtypeStruct((depth, b, n), F32),
        compiler_params=_cparams(("parallel", "parallel")),
        name="ada_modulation",
    )(c, ada_w, ada_b.reshape(depth, 1, n))


def _row_passes(rows, per=ROWS_PER_PASS):
    per = _tile(rows, per)
    return [slice(r * per, (r + 1) * per) for r in range(rows // per)]


def _inv_rms(load):
    x = load()
    return lax.rsqrt(jnp.mean(x * x, axis=-1, keepdims=True) + EPS)


def _norm_mod_rows(x_ref, rows, nw, mod_ref, shift_row):
    gain = nw * (1.0 + mod_ref[0, shift_row + 1:shift_row + 2, :])
    shift = mod_ref[0, shift_row:shift_row + 1, :]
    load = lambda: x_ref[0, rows, :]
    return load() * _inv_rms(load) * gain + shift


def _proj_kernel(x_ref, nw_ref, mod_ref, w_ref, *rest, shift_row, with_tail):
    if with_tail:
        wt_ref, o_ref, ot_ref, h_ref = rest
    else:
        o_ref, h_ref = rest
    tm = h_ref.shape[0]

    def body(first):
        for rows in _row_passes(tm, 512):
            if first:
                for sub in _row_passes(rows.stop - rows.start):
                    rr = slice(rows.start + sub.start, rows.start + sub.stop)
                    h_ref[rr, :] = _norm_mod_rows(x_ref, rr, nw_ref[...], mod_ref, shift_row).astype(BF16)
            o_ref[0, rows, :] = jnp.dot(h_ref[rows, :], w_ref[...],
                                        preferred_element_type=F32).astype(o_ref.dtype)
            if first and with_tail:
                ot_ref[0, rows, :] = jnp.dot(h_ref[rows, :], wt_ref[...], preferred_element_type=F32)

    @pl.when(pl.program_id(2) == 0)
    def _():
        body(True)

    @pl.when(pl.program_id(2) > 0)
    def _():
        body(False)


def norm_mod_project(x, nw, mod, w, layer, shift_row, out_dtype, n=None, w_tail=None):
    b, s, d = x.shape
    n = w.shape[2] if n is None else n
    tm = _tile(s, 1024)
    tn = _tile(n, 2048)
    in_specs = [pl.BlockSpec((1, tm, d), lambda i, m, j: (i, m, 0)),
                pl.BlockSpec((1, d), lambda i, m, j: (0, 0)),
                pl.BlockSpec((1, N_MOD, d), lambda i, m, j: (i, 0, 0)),
                pl.BlockSpec((None, d, tn), lambda i, m, j: (layer, 0, j))]
    out_specs = pl.BlockSpec((1, tm, tn), lambda i, m, j: (i, m, j))
    out_shape = jax.ShapeDtypeStruct((b, s, n), out_dtype)
    args = (x, nw, mod, w)
    if w_tail is not None:
        nt = w_tail.shape[1]
        in_specs.append(pl.BlockSpec((d, nt), lambda i, m, j: (0, 0)))
        out_specs = [out_specs, pl.BlockSpec((1, tm, nt), lambda i, m, j: (i, m, 0))]
        out_shape = [out_shape, jax.ShapeDtypeStruct((b, s, nt), F32)]
        args = args + (w_tail,)
    return pl.pallas_call(
        functools.partial(_proj_kernel, shift_row=shift_row, with_tail=w_tail is not None),
        grid=(b, s // tm, n // tn),
        in_specs=in_specs,
        out_specs=out_specs,
        out_shape=out_shape,
        scratch_shapes=[pltpu.VMEM((tm, d), BF16)],
        compiler_params=_cparams(("parallel", "parallel", "arbitrary")),
        name="norm_mod_project",
    )(*args)


def _out_kernel(a_ref, w_ref, x_ref, nw_ref, mod_ref, o_ref, *, gate_row):
    o_ref[0] = jnp.dot(a_ref[0], w_ref[...], preferred_element_type=F32)
    gain = mod_ref[0, gate_row:gate_row + 1, :] * nw_ref[...]
    for rows in _row_passes(o_ref.shape[1]):
        load = lambda: o_ref[0, rows, :]
        o_ref[0, rows, :] = x_ref[0, rows, :] + load() * _inv_rms(load) * gain


def out_project_residual(a, w, layer, x, nw, mod, gate_row):
    b, s, k = a.shape
    d = w.shape[2]
    tm = _tile(s, 512)
    return pl.pallas_call(
        functools.partial(_out_kernel, gate_row=gate_row),
        grid=(b, s // tm),
        in_specs=[pl.BlockSpec((1, tm, k), lambda i, m: (i, m, 0)),
                  pl.BlockSpec((None, k, d), lambda i, m: (layer, 0, 0), pipeline_mode=pl.Buffered(1)),
                  pl.BlockSpec((1, tm, d), lambda i, m: (i, m, 0)),
                  pl.BlockSpec((1, d), lambda i, m: (0, 0)),
                  pl.BlockSpec((1, N_MOD, d), lambda i, m: (i, 0, 0))],
        out_specs=pl.BlockSpec((1, tm, d), lambda i, m: (i, m, 0)),
        out_shape=jax.ShapeDtypeStruct((b, s, d), F32),
        compiler_params=_cparams(("parallel", "parallel")),
        name="out_project_residual",
    )(a, w, x, nw, mod)


def _ffn_kernel(x_ref, nw_in_ref, mod_ref, wg_ref, wu_ref, wd_ref, nw_out_ref, o_ref, h_ref):
    k = pl.program_id(2)
    nk = pl.num_programs(2)
    tm = h_ref.shape[0]

    def body(first, last):
        for rows in _row_passes(tm, 512):
            if first:
                for sub in _row_passes(rows.stop - rows.start):
                    rr = slice(rows.start + sub.start, rows.start + sub.stop)
                    h_ref[rr, :] = _norm_mod_rows(x_ref, rr, nw_in_ref[...], mod_ref, 3).astype(BF16)
            h = h_ref[rows, :]
            g = jnp.dot(h, wg_ref[...], preferred_element_type=F32)
            u = jnp.dot(h, wu_ref[...], preferred_element_type=F32)
            part = jnp.dot((_silu(g) * u).astype(BF16), wd_ref[...], preferred_element_type=F32)
            if first:
                o_ref[0, rows, :] = part
            else:
                o_ref[0, rows, :] += part
            if last:
                gain = mod_ref[0, 5:6, :] * nw_out_ref[...]
                for sub in _row_passes(rows.stop - rows.start):
                    rr = slice(rows.start + sub.start, rows.start + sub.stop)
                    load = lambda: o_ref[0, rr, :]
                    o_ref[0, rr, :] = x_ref[0, rr, :] + load() * _inv_rms(load) * gain

    @pl.when((k == 0) & (nk > 1))
    def _():
        body(True, False)

    @pl.when((k > 0) & (k < nk - 1))
    def _():
        body(False, False)

    @pl.when((k == nk - 1) & (nk > 1))
    def _():
        body(False, True)

    @pl.when(nk == 1)
    def _():
        body(True, True)


def ffn_residual(x, nw_in, nw_out, mod, w_gate_up, w_down, layer):
    b, s, d = x.shape
    f = w_down.shape[1]
    tm = _tile(s, 1024)
    tf = _tile(f, 512)
    nf = f // tf
    return pl.pallas_call(
        _ffn_kernel,
        grid=(b, s // tm, nf),
        in_specs=[pl.BlockSpec((1, tm, d), lambda i, m, k: (i, m, 0)),
                  pl.BlockSpec((1, d), lambda i, m, k: (0, 0)),
                  pl.BlockSpec((1, N_MOD, d), lambda i, m, k: (i, 0, 0)),
                  pl.BlockSpec((None, d, tf), lambda i, m, k: (layer, 0, k)),
                  pl.BlockSpec((None, d, tf), lambda i, m, k: (layer, 0, k + nf)),
                  pl.BlockSpec((None, tf, d), lambda i, m, k: (layer, k, 0)),
                  pl.BlockSpec((1, d), lambda i, m, k: (0, 0))],
        out_specs=pl.BlockSpec((1, tm, d), lambda i, m, k: (i, m, 0)),
        out_shape=jax.ShapeDtypeStruct((b, s, d), F32),
        scratch_shapes=[pltpu.VMEM((tm, d), BF16)],
        compiler_params=_cparams(("parallel", "parallel", "arbitrary")),
        name="ffn_residual",
    )(x, nw_in, mod, w_gate_up, w_gate_up, w_down, nw_out)


def _row_group_bcast(x, group, row):
    c, w = x.shape
    x3 = x.reshape(c // group, group, w)
    return jnp.broadcast_to(x3[:, row:row + 1, :], x3.shape).reshape(c, w)


def _hgrn_kernel(q_ref, f_ref, v_ref, g_ref, lbl_ref, nw_ref, o_ref, state_ref, kleaf_ref, bleaf_ref,
                 *, layer_j, hg):
    c = HG_CHUNK
    hd = HEAD_DIM
    n_chunks = q_ref.shape[1] // c
    pairs = [(j, m) for j in range(n_chunks) for m in range(hg)]

    @pl.when(pl.program_id(2) == 0)
    def _():
        state_ref[...] = jnp.zeros_like(state_ref)

    log_lb, log_1mlb = [], []
    for m in range(hg):
        logits = lbl_ref[m]
        e = jnp.exp(logits - jnp.max(logits, axis=0, keepdims=True))
        sm = e / jnp.sum(e, axis=0, keepdims=True)
        lb = jnp.sum(sm[:layer_j + 1], axis=0, keepdims=True) - sm[0:1]
        log_lb.append(jnp.log(lb))
        log_1mlb.append(jnp.log1p(-lb))

    row = lax.broadcasted_iota(jnp.int32, (c, c), 0)
    col = lax.broadcasted_iota(jnp.int32, (c, c), 1)
    rowk = lax.broadcasted_iota(jnp.int32, (c, hd), 0)
    tril = (row >= col).astype(BF16)

    def blk(ref, j, m):
        return ref[0, j * c:(j + 1) * c, m * hd:(m + 1) * hd]

    q = {(j, m): _silu(blk(q_ref, j, m)) for j, m in pairs}
    v = {(j, m): blk(v_ref, j, m) for j, m in pairs}
    log_f = {}
    for j, m in pairs:
        f_raw = blk(f_ref, j, m)
        log_sig = jnp.minimum(f_raw, 0.0) - jnp.log(1.0 + jnp.exp(-jnp.abs(f_raw)))
        cand = log_1mlb[m] + log_sig
        log_f[j, m] = jnp.maximum(log_lb[m], cand) + jnp.log(1.0 + jnp.exp(-jnp.abs(log_lb[m] - cand)))
    k = {jm: 1.0 - jnp.exp(log_f[jm]) for jm in pairs}
    b = {jm: _split_dot(tril, log_f[jm]) * LOG2E for jm in pairs}
    b_last = {jm: b[jm][c - 1:c, :] for jm in pairs}

    scores = {jm: jnp.zeros((c, c), F32) for jm in pairs}
    n = c
    while n > HG_LEAF:
        half = n // 2
        right = (rowk % n) >= half
        same = (row // n) == (col // n)
        qt, kt = {}, {}
        for jm in pairs:
            r = _row_group_bcast(b[jm], n, half - 1)
            qt[jm] = q[jm] * jnp.exp2(jnp.where(right, b[jm] - r, NEG))
            kt[jm] = k[jm] * jnp.exp2(jnp.where(right, NEG, r - b[jm]))
        scores = {jm: scores[jm] + jnp.where(same, _bdot_nt(qt[jm], kt[jm]), 0.0) for jm in pairs}
        n = half

    g = c // HG_LEAF
    t_in = lax.broadcasted_iota(jnp.int32, (g, HG_LEAF, hd), 1)
    col3 = lax.broadcasted_iota(jnp.int32, (g, HG_LEAF, c), 2)
    blk0 = lax.broadcasted_iota(jnp.int32, (g, HG_LEAF, c), 0) * HG_LEAF
    for p_idx, jm in enumerate(pairs):
        q3 = q[jm].reshape(g, HG_LEAF, hd)
        b3 = b[jm].reshape(g, HG_LEAF, hd)
        s3 = scores[jm].reshape(g, HG_LEAF, c)
        kleaf_ref[p_idx] = k[jm].reshape(g, HG_LEAF, hd)
        bleaf_ref[p_idx] = b3
        for s in range(HG_LEAF):
            ks = kleaf_ref[p_idx, :, pl.ds(s, 1), :]
            bs = bleaf_ref[p_idx, :, pl.ds(s, 1), :]
            p = q3 * ks * jnp.exp2(jnp.where(t_in >= s, b3 - bs, NEG))
            s3 = jnp.where(col3 == blk0 + s, jnp.sum(p, axis=-1, keepdims=True), s3)
        scores[jm] = s3.reshape(c, c)
    o = {jm: _bdot(scores[jm], v[jm]) for jm in pairs}

    qe = {jm: (q[jm] * jnp.exp2(b[jm])).astype(BF16) for jm in pairs}
    upd = {jm: _bdot_tn(v[jm], k[jm] * jnp.exp2(b_last[jm] - b[jm])) for jm in pairs}

    state = [state_ref[m] for m in range(hg)]
    for j in range(n_chunks):
        for m in range(hg):
            o[j, m] = o[j, m] + _bdot_nt(qe[j, m], state[m])
            state[m] = state[m] * jnp.exp2(b_last[j, m]) + upd[j, m]
    for m in range(hg):
        state_ref[m] = state[m]

    nw = nw_ref[...]
    for j, m in pairs:
        gate = _sigmoid(blk(g_ref, j, m))
        o_ref[0, j * c:(j + 1) * c, m * hd:(m + 1) * hd] = (_rms(o[j, m], nw) * gate).astype(o_ref.dtype)


def hgrn2_recurrence(proj, lb_logits, norm_w, layer_j):
    b, s, n4 = proj.shape
    hd = n4 // 4
    heads = hd // HEAD_DIM
    hg = 4 if heads % 4 == 0 else 1
    steps = heads // hg
    tc = _tile(s, 512)
    nl = lb_logits.shape[0]
    lbl = lb_logits.astype(F32).reshape(nl, heads, HEAD_DIM).transpose(1, 0, 2)
    leaf_shape = (hg * (tc // HG_CHUNK), HG_CHUNK // HG_LEAF, HG_LEAF, HEAD_DIM)

    def col(section):
        return pl.BlockSpec((1, tc, hg * HEAD_DIM), lambda i, h, t: (i, t, section * steps + h))

    return pl.pallas_call(
        functools.partial(_hgrn_kernel, layer_j=layer_j, hg=hg),
        grid=(b, steps, s // tc),
        in_specs=[col(0), col(1), col(2), col(3),
                  pl.BlockSpec((hg, nl, HEAD_DIM), lambda i, h, t: (h, 0, 0)),
                  pl.BlockSpec((1, HEAD_DIM), lambda i, h, t: (0, 0))],
        out_specs=pl.BlockSpec((1, tc, hg * HEAD_DIM), lambda i, h, t: (i, t, h)),
        out_shape=jax.ShapeDtypeStruct((b, s, hd), BF16),
        scratch_shapes=[pltpu.VMEM((hg, HEAD_DIM, HEAD_DIM), F32),
                        pltpu.VMEM(leaf_shape, F32),
                        pltpu.VMEM(leaf_shape, F32)],
        compiler_params=_cparams(("parallel", "parallel", "arbitrary")),
        name="hgrn2_recurrence",
    )(proj, proj, proj, proj, lbl, norm_w.astype(F32).reshape(1, HEAD_DIM))


def _l2n(x):
    return x * lax.rsqrt(jnp.sum(x * x, axis=-1, keepdims=True) + EPS)


def _gdn_gates_kernel(ba_ref, ga_ref, o_ref, *, v_heads):
    ba = ba_ref[0]
    t = ba.shape[0]
    beta = _sigmoid(ba)
    xa = ba + ga_ref[1:2, :]
    g = -jnp.exp(ga_ref[0:1, :]) * (jnp.maximum(xa, 0.0) + jnp.log1p(jnp.exp(-jnp.abs(xa))))
    row = lax.broadcasted_iota(jnp.int32, (t, t), 0)
    col = lax.broadcasted_iota(jnp.int32, (t, t), 1)
    block_tril = jnp.where((row >= col) & (row // GDN_CHUNK == col // GDN_CHUNK), 1.0, 0.0).astype(BF16)
    gc = _split_dot(block_tril, g)
    lane = lax.broadcasted_iota(jnp.int32, ba.shape, 1)
    o_ref[0] = jnp.where(lane < v_heads, beta, gc)


def gdn_gates(ba, a_log, dt_bias, v_heads):
    b, s, w = ba.shape
    tc = _tile(s, 512)
    pad = w - 2 * v_heads
    zeros = jnp.zeros((v_heads,), F32)
    gate_params = jnp.stack([
        jnp.concatenate([zeros, a_log.astype(F32), jnp.zeros((pad,), F32)]),
        jnp.concatenate([zeros, dt_bias.astype(F32), jnp.zeros((pad,), F32)]),
    ])
    return pl.pallas_call(
        functools.partial(_gdn_gates_kernel, v_heads=v_heads),
        grid=(b, s // tc),
        in_specs=[pl.BlockSpec((1, tc, w), lambda i, t: (i, t, 0)),
                  pl.BlockSpec((2, w), lambda i, t: (0, 0))],
        out_specs=pl.BlockSpec((1, tc, w), lambda i, t: (i, t, 0)),
        out_shape=jax.ShapeDtypeStruct((b, s, w), F32),
        compiler_params=_cparams(("parallel", "parallel")),
        name="gdn_gates",
    )(ba, gate_params)


def _conv_silu(xp_ref, w, lanes, nrows):
    out = xp_ref[SUBLANES:SUBLANES + nrows, lanes] * w[GDN_CONV - 1:GDN_CONV, :]
    for back in range(1, GDN_CONV):
        out = out + xp_ref[pl.ds(SUBLANES - back, nrows), lanes] * w[GDN_CONV - 1 - back:GDN_CONV - back, :]
    return _silu(out)


def _gdn_kernel(q_ref, k_ref, v_ref, z_ref, gates_ref, cq_ref, ck_ref, cv_ref, nw_ref,
                o_ref, state_ref, xq_ref, xk_ref, xv_ref, gc_ref,
                w2_ref, n_ref, qp_ref, o0_ref, *, kg, rep, v_heads):
    tc = q_ref.shape[1]
    hd = HEAD_DIM
    c = GDN_CHUNK
    n_chunks = tc // c
    nh = kg * rep

    @pl.when(pl.program_id(2) == 0)
    def _():
        state_ref[...] = jnp.zeros_like(state_ref)
        xq_ref[tc:, :] = jnp.zeros((SUBLANES, xq_ref.shape[1]), F32)
        xk_ref[tc:, :] = jnp.zeros((SUBLANES, xk_ref.shape[1]), F32)
        xv_ref[tc:, :] = jnp.zeros((SUBLANES, xv_ref.shape[1]), F32)

    row = lax.broadcasted_iota(jnp.int32, (c, c), 0)
    col = lax.broadcasted_iota(jnp.int32, (c, c), 1)
    eye = (row == col).astype(F32)
    rows = [slice(j * c, (j + 1) * c) for j in range(n_chunks)]

    for xp_ref, raw_ref in ((xq_ref, q_ref), (xk_ref, k_ref), (xv_ref, v_ref)):
        xp_ref[0:SUBLANES, :] = xp_ref[tc:tc + SUBLANES, :]
        xp_ref[SUBLANES:, :] = raw_ref[0].astype(F32)

    def prepare(khs):
        q, k, v, beta, gcb = {}, {}, {}, {}, {}
        gates = gates_ref[0]
        lane = lax.broadcasted_iota(jnp.int32, gates.shape, 1)
        for g in khs:
            lanes = slice(g * hd, (g + 1) * hd)
            qn = _l2n(_conv_silu(xq_ref, cq_ref[:, lanes], lanes, tc)) * (hd ** -0.5)
            kn = _l2n(_conv_silu(xk_ref, ck_ref[:, lanes], lanes, tc))
            for j in range(n_chunks):
                q[j, g] = qn[rows[j], :]
                k[j, g] = kn[rows[j], :]
            for m in range(g * rep, (g + 1) * rep):
                vlanes = slice(m * hd, (m + 1) * hd)
                va = _conv_silu(xv_ref, cv_ref[:, vlanes], vlanes, tc)
                vh = pl.program_id(1) * nh + m
                beta_col = jnp.sum(jnp.where(lane == vh, gates, 0.0), axis=-1, keepdims=True)
                gc_col = jnp.sum(jnp.where(lane == vh + v_heads, gates, 0.0), axis=-1, keepdims=True)
                beta_b = jnp.broadcast_to(beta_col, (tc, hd))
                gc_b = jnp.broadcast_to(gc_col, (tc, hd))
                gc_ref[m] = gc_b
                for j in range(n_chunks):
                    v[j, m] = va[rows[j], :]
                    beta[j, m] = beta_b[rows[j], :]
                    gcb[j, m] = gc_b[rows[j], :]
        return q, k, v, beta, gcb

    def solve(khs, prepared):
        q, k, v, beta, gcb = prepared
        pairs = [(j, m) for j in range(n_chunks) for g in khs for m in range(g * rep, (g + 1) * rep)]
        kpairs = [(j, g) for j in range(n_chunks) for g in khs]
        kb = {jg: k[jg].astype(BF16) for jg in kpairs}
        kk = {jg: _bdot_nt(kb[jg], kb[jg]) for jg in kpairs}
        qk = {jg: _bdot_nt(q[jg], kb[jg]) for jg in kpairs}
        decay = {jm: jnp.exp(jnp.where(row >= col, gcb[jm][:, 0:c] - jnp.transpose(gcb[jm])[0:1, :], NEG))
                 for jm in pairs}
        a = {(j, m): beta[j, m][:, 0:c] * kk[j, m // rep] * decay[j, m] for j, m in pairs}
        t_inv = {jm: eye - jnp.where((row // 2 == col // 2) & (row > col), a[jm], 0.0) for jm in pairs}
        ab = {jm: a[jm].astype(BF16) for jm in pairs}
        tb = {jm: t_inv[jm].astype(BF16) for jm in pairs}
        zero = jnp.zeros((c, c), BF16)
        node = 4
        while node <= c:
            links = (row // node == col // node) & (row % node >= node // 2) & (col % node < node // 2)
            et = {jm: jnp.dot(jnp.where(links, ab[jm], zero), tb[jm], preferred_element_type=F32).astype(BF16)
                  for jm in pairs}
            t_inv = {jm: t_inv[jm] - jnp.dot(tb[jm], et[jm], preferred_element_type=F32) for jm in pairs}
            tb = {jm: t_inv[jm].astype(BF16) for jm in pairs}
            node *= 2
        e_gc = {jm: jnp.exp(gcb[jm]) for jm in pairs}
        ub = {jm: _bdot(tb[jm], v[jm] * beta[jm]).astype(BF16) for jm in pairs}
        wb = {(j, m): _bdot(tb[j, m], k[j, m // rep] * (beta[j, m] * e_gc[j, m])).astype(BF16)
              for j, m in pairs}
        attn = {(j, m): (qk[j, m // rep] * decay[j, m]).astype(BF16) for j, m in pairs}
        kdt = {(j, m): jnp.transpose(k[j, m // rep] * jnp.exp(gcb[j, m][c - 1:c, :] - gcb[j, m])).astype(BF16)
               for j, m in pairs}
        for j, m in pairs:
            w2_ref[m, j] = jnp.dot(kdt[j, m], wb[j, m], preferred_element_type=F32).astype(BF16)
        for j, m in pairs:
            n_ref[m, j] = jnp.dot(kdt[j, m], ub[j, m], preferred_element_type=F32)
        for j, m in pairs:
            qp_ref[m, rows[j], :] = (q[j, m // rep] * e_gc[j, m]
                                     - jnp.dot(attn[j, m], wb[j, m], preferred_element_type=F32)).astype(BF16)
        for j, m in pairs:
            o0_ref[m, rows[j], :] = jnp.dot(attn[j, m], ub[j, m], preferred_element_type=F32)

    half = max(kg // 2, 1)
    groups = [range(g0, min(g0 + half, kg)) for g0 in range(0, kg, half)]
    prepared = [prepare(khs) for khs in groups]
    for khs, prep in zip(groups, prepared):
        solve(khs, prep)

    heads = range(nh)
    nw = nw_ref[...]

    def advance(i):
        sl = pl.ds(pl.multiple_of(i * c, c), c)
        state = [state_ref[m] for m in heads]
        sb = [x.astype(BF16) for x in state]
        ws = [jnp.dot(w2_ref[m, i], sb[m], preferred_element_type=F32) for m in heads]
        o = [jnp.dot(qp_ref[m, sl, :], sb[m], preferred_element_type=F32) for m in heads]
        for m in heads:
            g_last = gc_ref[m, pl.ds(i * c + c - 1, 1), :]
            state_ref[m] = state[m] * jnp.exp(g_last) - ws[m] + n_ref[m, i]
        for m in heads:
            o0_ref[m, sl, :] += o[m]

    def finish(i):
        sl = pl.ds(pl.multiple_of(i * c, c), c)
        for m in heads:
            lanes = slice(m * hd, (m + 1) * hd)
            z = z_ref[0, sl, lanes].astype(F32)
            o_ref[0, sl, lanes] = (_rms(o0_ref[m, sl, :], nw) * _silu(z)).astype(o_ref.dtype)

    advance(0)

    def body(i, carry):
        finish(i - 1)
        advance(i)
        return carry

    lax.fori_loop(1, n_chunks, body, 0, unroll=True)
    finish(n_chunks - 1)


def gdn_recurrence(proj, gates, conv_w, norm_w, k_heads, v_heads):
    b, s, _ = proj.shape
    rep = v_heads // k_heads
    kg = 4 if k_heads % 4 == 0 else 1
    nh = kg * rep
    qw = kg * HEAD_DIM
    vw = nh * HEAD_DIM
    tc = _tile(s, 512)
    c = GDN_CHUNK
    n_chunks = tc // c
    steps = k_heads // kg
    v0 = 2 * k_heads * HEAD_DIM // vw
    z0 = v0 + v_heads * HEAD_DIM // vw
    conv_w = conv_w.astype(F32)

    return pl.pallas_call(
        functools.partial(_gdn_kernel, kg=kg, rep=rep, v_heads=v_heads),
        grid=(b, steps, s // tc),
        in_specs=[pl.BlockSpec((1, tc, qw), lambda i, h, t: (i, t, h)),
                  pl.BlockSpec((1, tc, qw), lambda i, h, t: (i, t, steps + h)),
                  pl.BlockSpec((1, tc, vw), lambda i, h, t: (i, t, v0 + h)),
                  pl.BlockSpec((1, tc, vw), lambda i, h, t: (i, t, z0 + h)),
                  pl.BlockSpec((1, tc, HEAD_DIM), lambda i, h, t: (i, t, 0)),
                  pl.BlockSpec((GDN_CONV, qw), lambda i, h, t: (0, h)),
                  pl.BlockSpec((GDN_CONV, qw), lambda i, h, t: (0, steps + h)),
                  pl.BlockSpec((GDN_CONV, vw), lambda i, h, t: (0, v0 + h)),
                  pl.BlockSpec((1, HEAD_DIM), lambda i, h, t: (0, 0))],
        out_specs=pl.BlockSpec((1, tc, vw), lambda i, h, t: (i, t, h)),
        out_shape=jax.ShapeDtypeStruct((b, s, v_heads * HEAD_DIM), BF16),
        scratch_shapes=[pltpu.VMEM((nh, HEAD_DIM, HEAD_DIM), F32),
                        pltpu.VMEM((tc + SUBLANES, qw), F32),
                        pltpu.VMEM((tc + SUBLANES, qw), F32),
                        pltpu.VMEM((tc + SUBLANES, vw), F32),
                        pltpu.VMEM((nh, tc, HEAD_DIM), F32),
                        pltpu.VMEM((nh, n_chunks, HEAD_DIM, HEAD_DIM), BF16),
                        pltpu.VMEM((nh, n_chunks, HEAD_DIM, HEAD_DIM), F32),
                        pltpu.VMEM((nh, tc, HEAD_DIM), BF16),
                        pltpu.VMEM((nh, tc, HEAD_DIM), F32)],
        compiler_params=_cparams(("parallel", "parallel", "arbitrary")),
        name="gdn_recurrence",
    )(proj, proj, proj, proj, gates, conv_w, conv_w, conv_w, norm_w.astype(F32).reshape(1, HEAD_DIM))


def kernel(x, c, ada_w, ada_b, norm_w, hg_w_in, hg_lb_logits, hg_norm_w, hg_w_out, gdn_w_in,
           gdn_conv_w, gdn_A_log, gdn_dt_bias, gdn_norm_w, gdn_w_out, ffn_w_gate_up, ffn_w_down):
    depth = ada_w.shape[0]
    b, s, d = x.shape
    v_heads = gdn_A_log.shape[1]
    k_heads = v_heads // 2
    gdn_main = 2 * k_heads * HEAD_DIM + 2 * v_heads * HEAD_DIM

    mod_all = ada_modulation(c, ada_w, ada_b).reshape(depth, b, N_MOD, d)
    hg_in, hg_out = hg_w_in.astype(BF16), hg_w_out.astype(BF16)
    gdn_in, gdn_out = gdn_w_in.astype(BF16), gdn_w_out.astype(BF16)
    ffn_gu, ffn_dn = ffn_w_gate_up.astype(BF16), ffn_w_down.astype(BF16)
    for layer in range(depth):
        mod = mod_all[layer]
        nw = norm_w[layer]
        j = layer // 2
        if layer % 2 == 0:
            proj = norm_mod_project(x, nw[0:1], mod, hg_in, j, 0, F32)
            mixed = hgrn2_recurrence(proj, hg_lb_logits, hg_norm_w[j], j)
            w_out = hg_out
        else:
            w_tail = jnp.pad(gdn_w_in[j][:, gdn_main:], ((0, 0), (0, HEAD_DIM - 2 * v_heads)))
            proj, ba = norm_mod_project(x, nw[0:1], mod, gdn_in, j, 0, BF16, n=gdn_main,
                                        w_tail=w_tail.astype(BF16))
            gates = gdn_gates(ba, gdn_A_log[j], gdn_dt_bias[j], v_heads)
            mixed = gdn_recurrence(proj, gates, gdn_conv_w[j], gdn_norm_w[j], k_heads, v_heads)
            w_out = gdn_out
        x = out_project_residual(mixed, w_out, j, x, nw[1:2], mod, 2)
        x = ffn_residual(x, nw[2:3], nw[3:4], mod, ffn_gu, ffn_dn, layer)
    return x
```

```python
import functools

import jax
import jax.numpy as jnp
from jax import lax
from jax.experimental import pallas as pl
from jax.experimental.pallas import tpu as pltpu

F32 = jnp.float32
BF16 = jnp.bfloat16
EPS = 1e-6
N_MOD = 6
HEAD_DIM = 128
GDN_CONV = 4
SUBLANES = 8
GDN_CHUNK = 64
HG_CHUNK = 128
HG_LEAF = 8
NEG = -1e30
LOG2E = 1.4426950408889634
ROWS_PER_PASS = 256
VMEM_LIMIT = 56 * 1024 * 1024


def _cparams(sem):
    return pltpu.CompilerParams(dimension_semantics=sem, vmem_limit_bytes=VMEM_LIMIT)


def _tile(n, pref):
    t = min(n, pref)
    while n % t:
        t //= 2
    return t


def _sigmoid(x):
    return 0.5 * jnp.tanh(0.5 * x) + 0.5


def _silu(x):
    h = 0.5 * x
    return h * jnp.tanh(h) + h


def _bdot(a, b):
    return jnp.dot(a.astype(BF16), b.astype(BF16), preferred_element_type=F32)


def _bdot_nt(a, b):
    return lax.dot_general(a.astype(BF16), b.astype(BF16), (((1,), (1,)), ((), ())),
                           preferred_element_type=F32)


def _bdot_tn(a, b):
    return lax.dot_general(a.astype(BF16), b.astype(BF16), (((0,), (0,)), ((), ())),
                           preferred_element_type=F32)


def _split_dot(a_exact_bf16, x):
    hi = x.astype(BF16)
    lo = (x - hi.astype(F32)).astype(BF16)
    return (jnp.dot(a_exact_bf16, hi, preferred_element_type=F32)
            + jnp.dot(a_exact_bf16, lo, preferred_element_type=F32))


def _rms(x, w):
    ms = jnp.mean(x * x, axis=-1, keepdims=True)
    return x * lax.rsqrt(ms + EPS) * w


def _ada_kernel(c_ref, w_ref, b_ref, o_ref):
    c = c_ref[...]
    o_ref[0] = _bdot(_silu(c), w_ref[0]) + b_ref[0]


def ada_modulation(c, ada_w, ada_b):
    depth, d, n = ada_w.shape
    b = c.shape[0]
    tn = _tile(n, 2048)
    return pl.pallas_call(
        _ada_kernel,
        grid=(depth, n // tn),
        in_specs=[pl.BlockSpec((b, d), lambda l, j: (0, 0)),
                  pl.BlockSpec((1, d, tn), lambda l, j: (l, 0, j)),
                  pl.BlockSpec((1, 1, tn), lambda l, j: (l, 0, j))],
        out_specs=pl.BlockSpec((1, b, tn), lambda l, j: (l, 0, j)),
        out_shape=jax.ShapeDtypeStruct((depth, b, n), F32),
        compiler_params=_cparams(("parallel", "parallel")),
        name="ada_modulation",
    )(c, ada_w, ada_b.reshape(depth, 1, n))


def _row_passes(rows, per=ROWS_PER_PASS):
    per = _tile(rows, per)
    return [slice(r * per, (r + 1) * per) for r in range(rows // per)]


def _inv_rms(load):
    x = load()
    return lax.rsqrt(jnp.mean(x * x, axis=-1, keepdims=True) + EPS)


def _norm_mod_rows(x_ref, rows, nw, mod_ref, shift_row):
    gain = nw * (1.0 + mod_ref[0, shift_row + 1:shift_row + 2, :])
    shift = mod_ref[0, shift_row:shift_row + 1, :]
    load = lambda: x_ref[0, rows, :]
    return load() * _inv_rms(load) * gain + shift


def _gdn_gate_math(ba, ga_ref, v_heads):
    t = ba.shape[0]
    beta = _sigmoid(ba)
    xa = ba + ga_ref[1:2, :]
    g = -jnp.exp(ga_ref[0:1, :]) * (jnp.maximum(xa, 0.0) + jnp.log1p(jnp.exp(-jnp.abs(xa))))
    row = lax.broadcasted_iota(jnp.int32, (t, t), 0)
    col = lax.broadcasted_iota(jnp.int32, (t, t), 1)
    block_tril = jnp.where((row >= col) & (row // GDN_CHUNK == col // GDN_CHUNK), 1.0, 0.0).astype(BF16)
    gc = _split_dot(block_tril, g)
    lane = lax.broadcasted_iota(jnp.int32, ba.shape, 1)
    return jnp.where(lane < v_heads, beta, gc)


def _proj_kernel(x_ref, nw_ref, mod_ref, w_ref, *rest, shift_row, with_tail):
    if with_tail:
        wt_ref, ga_ref, o_ref, ot_ref, h_ref = rest
    else:
        o_ref, h_ref = rest
    tm = h_ref.shape[0]

    def body(first):
        for rows in _row_passes(tm, 512):
            if first:
                for sub in _row_passes(rows.stop - rows.start):
                    rr = slice(rows.start + sub.start, rows.start + sub.stop)
                    h_ref[rr, :] = _norm_mod_rows(x_ref, rr, nw_ref[...], mod_ref, shift_row).astype(BF16)
            o_ref[0, rows, :] = jnp.dot(h_ref[rows, :], w_ref[...],
                                        preferred_element_type=F32).astype(o_ref.dtype)
            if first and with_tail:
                ba = jnp.dot(h_ref[rows, :], wt_ref[...], preferred_element_type=F32)
                ot_ref[0, rows, :] = _gdn_gate_math(ba, ga_ref, with_tail)

    @pl.when(pl.program_id(2) == 0)
    def _():
        body(True)

    @pl.when(pl.program_id(2) > 0)
    def _():
        body(False)


def norm_mod_project(x, nw, mod, w, layer, shift_row, out_dtype, n=None, w_tail=None, gate_params=None,
                     v_heads=0):
    b, s, d = x.shape
    n = w.shape[2] if n is None else n
    tm = _tile(s, 1024)
    tn = _tile(n, 2048)
    in_specs = [pl.BlockSpec((1, tm, d), lambda i, m, j: (i, m, 0)),
                pl.BlockSpec((1, d), lambda i, m, j: (0, 0)),
                pl.BlockSpec((1, N_MOD, d), lambda i, m, j: (i, 0, 0)),
                pl.BlockSpec((None, d, tn), lambda i, m, j: (layer, 0, j))]
    out_specs = pl.BlockSpec((1, tm, tn), lambda i, m, j: (i, m, j))
    out_shape = jax.ShapeDtypeStruct((b, s, n), out_dtype)
    args = (x, nw, mod, w)
    if w_tail is not None:
        nt = w_tail.shape[1]
        in_specs.append(pl.BlockSpec((d, nt), lambda i, m, j: (0, 0)))
        in_specs.append(pl.BlockSpec((2, nt), lambda i, m, j: (0, 0)))
        out_specs = [out_specs, pl.BlockSpec((1, tm, nt), lambda i, m, j: (i, m, 0))]
        out_shape = [out_shape, jax.ShapeDtypeStruct((b, s, nt), F32)]
        args = args + (w_tail, gate_params)
    return pl.pallas_call(
        functools.partial(_proj_kernel, shift_row=shift_row, with_tail=v_heads if w_tail is not None else 0),
        grid=(b, s // tm, n // tn),
        in_specs=in_specs,
        out_specs=out_specs,
        out_shape=out_shape,
        scratch_shapes=[pltpu.VMEM((tm, d), BF16)],
        compiler_params=_cparams(("parallel", "parallel", "arbitrary")),
        name="norm_mod_project",
    )(*args)


def _out_kernel(a_ref, w_ref, x_ref, nw_ref, mod_ref, o_ref, *, gate_row):
    o_ref[0] = jnp.dot(a_ref[0], w_ref[...], preferred_element_type=F32)
    gain = mod_ref[0, gate_row:gate_row + 1, :] * nw_ref[...]
    for rows in _row_passes(o_ref.shape[1]):
        load = lambda: o_ref[0, rows, :]
        o_ref[0, rows, :] = x_ref[0, rows, :] + load() * _inv_rms(load) * gain


def out_project_residual(a, w, layer, x, nw, mod, gate_row):
    b, s, k = a.shape
    d = w.shape[2]
    tm = _tile(s, 512)
    return pl.pallas_call(
        functools.partial(_out_kernel, gate_row=gate_row),
        grid=(b, s // tm),
        in_specs=[pl.BlockSpec((1, tm, k), lambda i, m: (i, m, 0)),
                  pl.BlockSpec((None, k, d), lambda i, m: (layer, 0, 0), pipeline_mode=pl.Buffered(1)),
                  pl.BlockSpec((1, tm, d), lambda i, m: (i, m, 0)),
                  pl.BlockSpec((1, d), lambda i, m: (0, 0)),
                  pl.BlockSpec((1, N_MOD, d), lambda i, m: (i, 0, 0))],
        out_specs=pl.BlockSpec((1, tm, d), lambda i, m: (i, m, 0)),
        out_shape=jax.ShapeDtypeStruct((b, s, d), F32),
        compiler_params=_cparams(("parallel", "parallel")),
        name="out_project_residual",
    )(a, w, x, nw, mod)


def _ffn_kernel(x_ref, nw_in_ref, mod_ref, wg_ref, wu_ref, wd_ref, nw_out_ref, o_ref, h_ref):
    k = pl.program_id(2)
    nk = pl.num_programs(2)
    tm = h_ref.shape[0]

    def body(first, last):
        for rows in _row_passes(tm, 512):
            if first:
                for sub in _row_passes(rows.stop - rows.start):
                    rr = slice(rows.start + sub.start, rows.start + sub.stop)
                    h_ref[rr, :] = _norm_mod_rows(x_ref, rr, nw_in_ref[...], mod_ref, 3).astype(BF16)
            h = h_ref[rows, :]
            g = jnp.dot(h, wg_ref[...], preferred_element_type=F32)
            u = jnp.dot(h, wu_ref[...], preferred_element_type=F32)
            part = jnp.dot((_silu(g) * u).astype(BF16), wd_ref[...], preferred_element_type=F32)
            if first:
                o_ref[0, rows, :] = part
            else:
                o_ref[0, rows, :] += part
            if last:
                gain = mod_ref[0, 5:6, :] * nw_out_ref[...]
                for sub in _row_passes(rows.stop - rows.start):
                    rr = slice(rows.start + sub.start, rows.start + sub.stop)
                    load = lambda: o_ref[0, rr, :]
                    o_ref[0, rr, :] = x_ref[0, rr, :] + load() * _inv_rms(load) * gain

    @pl.when((k == 0) & (nk > 1))
    def _():
        body(True, False)

    @pl.when((k > 0) & (k < nk - 1))
    def _():
        body(False, False)

    @pl.when((k == nk - 1) & (nk > 1))
    def _():
        body(False, True)

    @pl.when(nk == 1)
    def _():
        body(True, True)


def ffn_residual(x, nw_in, nw_out, mod, w_gate_up, w_down, layer):
    b, s, d = x.shape
    f = w_down.shape[1]
    tm = _tile(s, 1024)
    tf = _tile(f, 512)
    nf = f // tf
    return pl.pallas_call(
        _ffn_kernel,
        grid=(b, s // tm, nf),
        in_specs=[pl.BlockSpec((1, tm, d), lambda i, m, k: (i, m, 0)),
                  pl.BlockSpec((1, d), lambda i, m, k: (0, 0)),
                  pl.BlockSpec((1, N_MOD, d), lambda i, m, k: (i, 0, 0)),
                  pl.BlockSpec((None, d, tf), lambda i, m, k: (layer, 0, k)),
                  pl.BlockSpec((None, d, tf), lambda i, m, k: (layer, 0, k + nf)),
                  pl.BlockSpec((None, tf, d), lambda i, m, k: (layer, k, 0)),
                  pl.BlockSpec((1, d), lambda i, m, k: (0, 0))],
        out_specs=pl.BlockSpec((1, tm, d), lambda i, m, k: (i, m, 0)),
        out_shape=jax.ShapeDtypeStruct((b, s, d), F32),
        scratch_shapes=[pltpu.VMEM((tm, d), BF16)],
        compiler_params=_cparams(("parallel", "parallel", "arbitrary")),
        name="ffn_residual",
    )(x, nw_in, mod, w_gate_up, w_gate_up, w_down, nw_out)


def _row_group_bcast(x, group, row):
    c, w = x.shape
    x3 = x.reshape(c // group, group, w)
    return jnp.broadcast_to(x3[:, row:row + 1, :], x3.shape).reshape(c, w)


def _hgrn_kernel(q_ref, f_ref, v_ref, g_ref, lbl_ref, nw_ref, o_ref, state_ref, kleaf_ref, bleaf_ref,
                 *, layer_j, hg):
    c = HG_CHUNK
    hd = HEAD_DIM
    n_chunks = q_ref.shape[1] // c
    pairs = [(j, m) for j in range(n_chunks) for m in range(hg)]

    @pl.when(pl.program_id(2) == 0)
    def _():
        state_ref[...] = jnp.zeros_like(state_ref)

    log_lb, log_1mlb = [], []
    for m in range(hg):
        logits = lbl_ref[m]
        e = jnp.exp(logits - jnp.max(logits, axis=0, keepdims=True))
        sm = e / jnp.sum(e, axis=0, keepdims=True)
        lb = jnp.sum(sm[:layer_j + 1], axis=0, keepdims=True) - sm[0:1]
        log_lb.append(jnp.log(lb))
        log_1mlb.append(jnp.log1p(-lb))

    row = lax.broadcasted_iota(jnp.int32, (c, c), 0)
    col = lax.broadcasted_iota(jnp.int32, (c, c), 1)
    rowk = lax.broadcasted_iota(jnp.int32, (c, hd), 0)
    tril = (row >= col).astype(BF16)

    def blk(ref, j, m):
        return ref[0, j * c:(j + 1) * c, m * hd:(m + 1) * hd]

    q = {(j, m): _silu(blk(q_ref, j, m)) for j, m in pairs}
    v = {(j, m): blk(v_ref, j, m) for j, m in pairs}
    log_f = {}
    for j, m in pairs:
        f_raw = blk(f_ref, j, m)
        log_sig = jnp.minimum(f_raw, 0.0) - jnp.log(1.0 + jnp.exp(-jnp.abs(f_raw)))
        cand = log_1mlb[m] + log_sig
        log_f[j, m] = jnp.maximum(log_lb[m], cand) + jnp.log(1.0 + jnp.exp(-jnp.abs(log_lb[m] - cand)))
    k = {jm: 1.0 - jnp.exp(log_f[jm]) for jm in pairs}
    b = {jm: _split_dot(tril, log_f[jm]) * LOG2E for jm in pairs}
    b_last = {jm: b[jm][c - 1:c, :] for jm in pairs}

    scores = {jm: jnp.zeros((c, c), F32) for jm in pairs}
    n = c
    while n > HG_LEAF:
        half = n // 2
        right = (rowk % n) >= half
        same = (row // n) == (col // n)
        qt, kt = {}, {}
        for jm in pairs:
            r = _row_group_bcast(b[jm], n, half - 1)
            qt[jm] = q[jm] * jnp.exp2(jnp.where(right, b[jm] - r, NEG))
            kt[jm] = k[jm] * jnp.exp2(jnp.where(right, NEG, r - b[jm]))
        scores = {jm: scores[jm] + jnp.where(same, _bdot_nt(qt[jm], kt[jm]), 0.0) for jm in pairs}
        n = half

    g = c // HG_LEAF
    t_in = lax.broadcasted_iota(jnp.int32, (g, HG_LEAF, hd), 1)
    col3 = lax.broadcasted_iota(jnp.int32, (g, HG_LEAF, c), 2)
    blk0 = lax.broadcasted_iota(jnp.int32, (g, HG_LEAF, c), 0) * HG_LEAF
    for p_idx, jm in enumerate(pairs):
        q3 = q[jm].reshape(g, HG_LEAF, hd)
        b3 = b[jm].reshape(g, HG_LEAF, hd)
        s3 = scores[jm].reshape(g, HG_LEAF, c)
        kleaf_ref[p_idx] = k[jm].reshape(g, HG_LEAF, hd)
        bleaf_ref[p_idx] = b3
        for s in range(HG_LEAF):
            ks = kleaf_ref[p_idx, :, pl.ds(s, 1), :]
            bs = bleaf_ref[p_idx, :, pl.ds(s, 1), :]
            p = q3 * ks * jnp.exp2(jnp.where(t_in >= s, b3 - bs, NEG))
            s3 = jnp.where(col3 == blk0 + s, jnp.sum(p, axis=-1, keepdims=True), s3)
        scores[jm] = s3.reshape(c, c)
    o = {jm: _bdot(scores[jm], v[jm]) for jm in pairs}

    qe = {jm: (q[jm] * jnp.exp2(b[jm])).astype(BF16) for jm in pairs}
    upd = {jm: _bdot_tn(v[jm], k[jm] * jnp.exp2(b_last[jm] - b[jm])) for jm in pairs}

    state = [state_ref[m] for m in range(hg)]
    for j in range(n_chunks):
        for m in range(hg):
            o[j, m] = o[j, m] + _bdot_nt(qe[j, m], state[m])
            state[m] = state[m] * jnp.exp2(b_last[j, m]) + upd[j, m]
    for m in range(hg):
        state_ref[m] = state[m]

    nw = nw_ref[...]
    for j, m in pairs:
        gate = _sigmoid(blk(g_ref, j, m))
        o_ref[0, j * c:(j + 1) * c, m * hd:(m + 1) * hd] = (_rms(o[j, m], nw) * gate).astype(o_ref.dtype)


def hgrn2_recurrence(proj, lb_logits, norm_w, layer_j):
    b, s, n4 = proj.shape
    hd = n4 // 4
    heads = hd // HEAD_DIM
    hg = 4 if heads % 4 == 0 else 1
    steps = heads // hg
    tc = _tile(s, 512)
    nl = lb_logits.shape[0]
    lbl = lb_logits.astype(F32).reshape(nl, heads, HEAD_DIM).transpose(1, 0, 2)
    leaf_shape = (hg * (tc // HG_CHUNK), HG_CHUNK // HG_LEAF, HG_LEAF, HEAD_DIM)

    def col(section):
        return pl.BlockSpec((1, tc, hg * HEAD_DIM), lambda i, h, t: (i, t, section * steps + h))

    return pl.pallas_call(
        functools.partial(_hgrn_kernel, layer_j=layer_j, hg=hg),
        grid=(b, steps, s // tc),
        in_specs=[col(0), col(1), col(2), col(3),
                  pl.BlockSpec((hg, nl, HEAD_DIM), lambda i, h, t: (h, 0, 0)),
                  pl.BlockSpec((1, HEAD_DIM), lambda i, h, t: (0, 0))],
        out_specs=pl.BlockSpec((1, tc, hg * HEAD_DIM), lambda i, h, t: (i, t, h)),
        out_shape=jax.ShapeDtypeStruct((b, s, hd), BF16),
        scratch_shapes=[pltpu.VMEM((hg, HEAD_DIM, HEAD_DIM), F32),
                        pltpu.VMEM(leaf_shape, F32),
                        pltpu.VMEM(leaf_shape, F32)],
        compiler_params=_cparams(("parallel", "parallel", "arbitrary")),
        name="hgrn2_recurrence",
    )(proj, proj, proj, proj, lbl, norm_w.astype(F32).reshape(1, HEAD_DIM))


def _l2n(x):
    return x * lax.rsqrt(jnp.sum(x * x, axis=-1, keepdims=True) + EPS)


def _conv_silu(xp_ref, w, lanes, nrows):
    out = xp_ref[SUBLANES:SUBLANES + nrows, lanes] * w[GDN_CONV - 1:GDN_CONV, :]
    for back in range(1, GDN_CONV):
        out = out + xp_ref[pl.ds(SUBLANES - back, nrows), lanes] * w[GDN_CONV - 1 - back:GDN_CONV - back, :]
    return _silu(out)


def _gdn_kernel(q_ref, k_ref, v_ref, z_ref, gates_ref, cq_ref, ck_ref, cv_ref, nw_ref,
                o_ref, state_ref, xq_ref, xk_ref, xv_ref, gc_ref,
                w2_ref, n_ref, qp_ref, o0_ref, *, kg, rep, v_heads):
    tc = q_ref.shape[1]
    hd = HEAD_DIM
    c = GDN_CHUNK
    n_chunks = tc // c
    nh = kg * rep

    @pl.when(pl.program_id(2) == 0)
    def _():
        state_ref[...] = jnp.zeros_like(state_ref)
        xq_ref[tc:, :] = jnp.zeros((SUBLANES, xq_ref.shape[1]), F32)
        xk_ref[tc:, :] = jnp.zeros((SUBLANES, xk_ref.shape[1]), F32)
        xv_ref[tc:, :] = jnp.zeros((SUBLANES, xv_ref.shape[1]), F32)

    row = lax.broadcasted_iota(jnp.int32, (c, c), 0)
    col = lax.broadcasted_iota(jnp.int32, (c, c), 1)
    eye = (row == col).astype(F32)
    rows = [slice(j * c, (j + 1) * c) for j in range(n_chunks)]

    for xp_ref, raw_ref in ((xq_ref, q_ref), (xk_ref, k_ref), (xv_ref, v_ref)):
        xp_ref[0:SUBLANES, :] = xp_ref[tc:tc + SUBLANES, :]
        xp_ref[SUBLANES:, :] = raw_ref[0].astype(F32)

    def prepare(khs):
        q, k, v, beta, gcb = {}, {}, {}, {}, {}
        gates = gates_ref[0]
        lane = lax.broadcasted_iota(jnp.int32, gates.shape, 1)
        for g in khs:
            lanes = slice(g * hd, (g + 1) * hd)
            qn = _l2n(_conv_silu(xq_ref, cq_ref[:, lanes], lanes, tc)) * (hd ** -0.5)
            kn = _l2n(_conv_silu(xk_ref, ck_ref[:, lanes], lanes, tc))
            for j in range(n_chunks):
                q[j, g] = qn[rows[j], :]
                k[j, g] = kn[rows[j], :]
            for m in range(g * rep, (g + 1) * rep):
                vlanes = slice(m * hd, (m + 1) * hd)
                va = _conv_silu(xv_ref, cv_ref[:, vlanes], vlanes, tc)
                vh = pl.program_id(1) * nh + m
                beta_col = jnp.sum(jnp.where(lane == vh, gates, 0.0), axis=-1, keepdims=True)
                gc_col = jnp.sum(jnp.where(lane == vh + v_heads, gates, 0.0), axis=-1, keepdims=True)
                beta_b = jnp.broadcast_to(beta_col, (tc, hd))
                gc_b = jnp.broadcast_to(gc_col, (tc, hd))
                gc_ref[m] = gc_b
                for j in range(n_chunks):
                    v[j, m] = va[rows[j], :]
                    beta[j, m] = beta_b[rows[j], :]
                    gcb[j, m] = gc_b[rows[j], :]
        return q, k, v, beta, gcb

    def solve(khs, prepared):
        q, k, v, beta, gcb = prepared
        pairs = [(j, m) for j in range(n_chunks) for g in khs for m in range(g * rep, (g + 1) * rep)]
        kpairs = [(j, g) for j in range(n_chunks) for g in khs]
        kb = {jg: k[jg].astype(BF16) for jg in kpairs}
        kk = {jg: _bdot_nt(kb[jg], kb[jg]) for jg in kpairs}
        qk = {jg: _bdot_nt(q[jg], kb[jg]) for jg in kpairs}
        decay = {jm: jnp.exp(jnp.where(row >= col, gcb[jm][:, 0:c] - jnp.transpose(gcb[jm])[0:1, :], NEG))
                 for jm in pairs}
        a = {(j, m): beta[j, m][:, 0:c] * kk[j, m // rep] * decay[j, m] for j, m in pairs}
        t_inv = {jm: eye - jnp.where((row // 2 == col // 2) & (row > col), a[jm], 0.0) for jm in pairs}
        ab = {jm: a[jm].astype(BF16) for jm in pairs}
        tb = {jm: t_inv[jm].astype(BF16) for jm in pairs}
        zero = jnp.zeros((c, c), BF16)
        node = 4
        while node <= c:
            links = (row // node == col // node) & (row % node >= node // 2) & (col % node < node // 2)
            et = {jm: jnp.dot(jnp.where(links, ab[jm], zero), tb[jm], preferred_element_type=F32).astype(BF16)
                  for jm in pairs}
            t_inv = {jm: t_inv[jm] - jnp.dot(tb[jm], et[jm], preferred_element_type=F32) for jm in pairs}
            tb = {jm: t_inv[jm].astype(BF16) for jm in pairs}
            node *= 2
        e_gc = {jm: jnp.exp(gcb[jm]) for jm in pairs}
        ub = {jm: _bdot(tb[jm], v[jm] * beta[jm]).astype(BF16) for jm in pairs}
        wb = {(j, m): _bdot(tb[j, m], k[j, m // rep] * (beta[j, m] * e_gc[j, m])).astype(BF16)
              for j, m in pairs}
        attn = {(j, m): (qk[j, m // rep] * decay[j, m]).astype(BF16) for j, m in pairs}
        kdt = {(j, m): jnp.transpose(k[j, m // rep] * jnp.exp(gcb[j, m][c - 1:c, :] - gcb[j, m])).astype(BF16)
               for j, m in pairs}
        for j, m in pairs:
            w2_ref[m, j] = jnp.dot(kdt[j, m], wb[j, m], preferred_element_type=F32).astype(BF16)
        for j, m in pairs:
            n_ref[m, j] = jnp.dot(kdt[j, m], ub[j, m], preferred_element_type=F32)
        for j, m in pairs:
            qp_ref[m, rows[j], :] = (q[j, m // rep] * e_gc[j, m]
                                     - jnp.dot(attn[j, m], wb[j, m], preferred_element_type=F32)).astype(BF16)
        for j, m in pairs:
            o0_ref[m, rows[j], :] = jnp.dot(attn[j, m], ub[j, m], preferred_element_type=F32)

    half = max(kg // 2, 1)
    groups = [range(g0, min(g0 + half, kg)) for g0 in range(0, kg, half)]
    prepared = [prepare(khs) for khs in groups]
    for khs, prep in zip(groups, prepared):
        solve(khs, prep)

    heads = range(nh)
    nw = nw_ref[...]

    def advance(i):
        sl = pl.ds(pl.multiple_of(i * c, c), c)
        state = [state_ref[m] for m in heads]
        sb = [x.astype(BF16) for x in state]
        ws = [jnp.dot(w2_ref[m, i], sb[m], preferred_element_type=F32) for m in heads]
        o = [jnp.dot(qp_ref[m, sl, :], sb[m], preferred_element_type=F32) for m in heads]
        for m in heads:
            g_last = gc_ref[m, pl.ds(i * c + c - 1, 1), :]
            state_ref[m] = state[m] * jnp.exp(g_last) - ws[m] + n_ref[m, i]
        for m in heads:
            o0_ref[m, sl, :] += o[m]

    def finish(i):
        sl = pl.ds(pl.multiple_of(i * c, c), c)
        for m in heads:
            lanes = slice(m * hd, (m + 1) * hd)
            z = z_ref[0, sl, lanes].astype(F32)
            o_ref[0, sl, lanes] = (_rms(o0_ref[m, sl, :], nw) * _silu(z)).astype(o_ref.dtype)

    advance(0)

    def body(i, carry):
        finish(i - 1)
        advance(i)
        return carry

    lax.fori_loop(1, n_chunks, body, 0, unroll=True)
    finish(n_chunks - 1)


def gdn_recurrence(proj, gates, conv_w, norm_w, k_heads, v_heads):
    b, s, _ = proj.shape
    rep = v_heads // k_heads
    kg = 4 if k_heads % 4 == 0 else 1
    nh = kg * rep
    qw = kg * HEAD_DIM
    vw = nh * HEAD_DIM
    tc = _tile(s, 512)
    c = GDN_CHUNK
    n_chunks = tc // c
    steps = k_heads // kg
    v0 = 2 * k_heads * HEAD_DIM // vw
    z0 = v0 + v_heads * HEAD_DIM // vw
    conv_w = conv_w.astype(F32)

    return pl.pallas_call(
        functools.partial(_gdn_kernel, kg=kg, rep=rep, v_heads=v_heads),
        grid=(b, steps, s // tc),
        in_specs=[pl.BlockSpec((1, tc, qw), lambda i, h, t: (i, t, h)),
                  pl.BlockSpec((1, tc, qw), lambda i, h, t: (i, t, steps + h)),
                  pl.BlockSpec((1, tc, vw), lambda i, h, t: (i, t, v0 + h)),
                  pl.BlockSpec((1, tc, vw), lambda i, h, t: (i, t, z0 + h)),
                  pl.BlockSpec((1, tc, HEAD_DIM), lambda i, h, t: (i, t, 0)),
                  pl.BlockSpec((GDN_CONV, qw), lambda i, h, t: (0, h)),
                  pl.BlockSpec((GDN_CONV, qw), lambda i, h, t: (0, steps + h)),
                  pl.BlockSpec((GDN_CONV, vw), lambda i, h, t: (0, v0 + h)),
                  pl.BlockSpec((1, HEAD_DIM), lambda i, h, t: (0, 0))],
        out_specs=pl.BlockSpec((1, tc, vw), lambda i, h, t: (i, t, h)),
        out_shape=jax.ShapeDtypeStruct((b, s, v_heads * HEAD_DIM), BF16),
        scratch_shapes=[pltpu.VMEM((nh, HEAD_DIM, HEAD_DIM), F32),
                        pltpu.VMEM((tc + SUBLANES, qw), F32),
                        pltpu.VMEM((tc + SUBLANES, qw), F32),
                        pltpu.VMEM((tc + SUBLANES, vw), F32),
                        pltpu.VMEM((nh, tc, HEAD_DIM), F32),
                        pltpu.VMEM((nh, n_chunks, HEAD_DIM, HEAD_DIM), BF16),
                        pltpu.VMEM((nh, n_chunks, HEAD_DIM, HEAD_DIM), F32),
                        pltpu.VMEM((nh, tc, HEAD_DIM), BF16),
                        pltpu.VMEM((nh, tc, HEAD_DIM), F32)],
        compiler_params=_cparams(("parallel", "parallel", "arbitrary")),
        name="gdn_recurrence",
    )(proj, proj, proj, proj, gates, conv_w, conv_w, conv_w, norm_w.astype(F32).reshape(1, HEAD_DIM))


def kernel(x, c, ada_w, ada_b, norm_w, hg_w_in, hg_lb_logits, hg_norm_w, hg_w_out, gdn_w_in,
           gdn_conv_w, gdn_A_log, gdn_dt_bias, gdn_norm_w, gdn_w_out, ffn_w_gate_up, ffn_w_down):
    depth = ada_w.shape[0]
    b, s, d = x.shape
    v_heads = gdn_A_log.shape[1]
    k_heads = v_heads // 2
    gdn_main = 2 * k_heads * HEAD_DIM + 2 * v_heads * HEAD_DIM

    mod_all = ada_modulation(c, ada_w, ada_b).reshape(depth, b, N_MOD, d)
    hg_in, hg_out = hg_w_in.astype(BF16), hg_w_out.astype(BF16)
    gdn_in, gdn_out = gdn_w_in.astype(BF16), gdn_w_out.astype(BF16)
    ffn_gu, ffn_dn = ffn_w_gate_up.astype(BF16), ffn_w_down.astype(BF16)
    for layer in range(depth):
        mod = mod_all[layer]
        nw = norm_w[layer]
        j = layer // 2
        if layer % 2 == 0:
            proj = norm_mod_project(x, nw[0:1], mod, hg_in, j, 0, F32)
            mixed = hgrn2_recurrence(proj, hg_lb_logits, hg_norm_w[j], j)
            w_out = hg_out
        else:
            w_tail = jnp.pad(gdn_w_in[j][:, gdn_main:], ((0, 0), (0, HEAD_DIM - 2 * v_heads)))
            zeros = jnp.zeros((v_heads,), F32)
            pad = jnp.zeros((HEAD_DIM - 2 * v_heads,), F32)
            gate_params = jnp.stack([jnp.concatenate([zeros, gdn_A_log[j].astype(F32), pad]),
                                     jnp.concatenate([zeros, gdn_dt_bias[j].astype(F32), pad])])
            proj, gates = norm_mod_project(x, nw[0:1], mod, gdn_in, j, 0, BF16, n=gdn_main,
                                           w_tail=w_tail.astype(BF16), gate_params=gate_params, v_heads=v_heads)
            mixed = gdn_recurrence(proj, gates, gdn_conv_w[j], gdn_norm_w[j], k_heads, v_heads)
            w_out = gdn_out
        x = out_project_residual(mixed, w_out, j, x, nw[1:2], mod, 2)
        x = ffn_residual(x, nw[2:3], nw[3:4], mod, ffn_gu, ffn_dn, layer)
    return x
```
